```python
import numpy as np
import jax, jax.numpy as jnp
from jax import lax

D_MODEL = 1024
BATCH = 32
SEQ = 2048
DEPTH = 1

POOL_WINDOWS = (2, 4, 8, 16)
N_POOL_GROUPS = len(POOL_WINDOWS)
POOL_GROUP = D_MODEL // 8
POOL_WIDTH = POOL_GROUP * N_POOL_GROUPS
HEAD_DIM = 64
N_HEADS = D_MODEL // HEAD_DIM
N_KV_GROUPS = 2
HEADS_PER_GROUP = N_HEADS // N_KV_GROUPS
ATTN_WIDTH = N_HEADS * HEAD_DIM
N_NSA_BRANCHES = 3
KV_WIDTH = N_NSA_BRANCHES * 2 * N_KV_GROUPS * HEAD_DIM
CMP_BLOCK = 32
CMP_STRIDE = 16
CMP_HIDDEN = 128
SLC_BLOCK = 64
SLC_TOP_N = 8
WINDOW = 512
Q_BLOCK = 32
ROPE_THETA = 10000.0
SCALE = HEAD_DIM ** -0.5
FORCE_BONUS = 1000.0
NEG_INF = -1e30
N_MERGE = 2
OFF_Q = POOL_WIDTH
OFF_KV = OFF_Q + ATTN_WIDTH
OFF_NSA_G = OFF_KV + KV_WIDTH
OFF_MERGE_G = OFF_NSA_G + N_NSA_BRANCHES * N_HEADS
IN_WIDTH = OFF_MERGE_G + N_MERGE * D_MODEL
D_FF = 2752
CONV_WIDTH = 3
EPS = 1e-6

kernel_name = "hybrid_pool_nsa_convffn"


def rms_norm(x, w):
    xf = x.astype(jnp.float32)
    y = xf * lax.rsqrt(jnp.mean(xf * xf, axis=-1, keepdims=True) + EPS)
    return (y * w.astype(jnp.float32)).astype(x.dtype)


def rope(x, pos):
    half = HEAD_DIM // 2
    freqs = ROPE_THETA ** (-jnp.arange(half, dtype=jnp.float32) / half)
    ang = pos.astype(jnp.float32)[:, None] * freqs[None, :]
    shp = (1, pos.shape[0]) + (1,) * (x.ndim - 3) + (half,)
    cos = jnp.cos(ang).reshape(shp)
    sin = jnp.sin(ang).reshape(shp)
    xf = x.astype(jnp.float32)
    x1, x2 = xf[..., :half], xf[..., half:]
    return jnp.concatenate([x1 * cos - x2 * sin, x2 * cos + x1 * sin], axis=-1).astype(x.dtype)


def masked_softmax(s, mask):
    s = jnp.where(mask, s.astype(jnp.float32), NEG_INF)
    p = jax.nn.softmax(s, axis=-1)
    return jnp.where(mask, p, 0.0)


def pool_mixer(u, pool_w, pool_scale):
    B, S, _ = u.shape
    ug = u.reshape(B, S, N_POOL_GROUPS, POOL_GROUP).astype(jnp.float32)
    c = jnp.pad(jnp.cumsum(ug, axis=1), ((0, 0), (1, 0), (0, 0), (0, 0)))
    t = jnp.arange(S)
    outs = []
    for gi, w in enumerate(POOL_WINDOWS):
        lo = jnp.maximum(t + 1 - w, 0)
        cnt = jnp.minimum(t + 1, w).astype(jnp.float32)
        outs.append((c[:, 1:, gi] - c[:, lo, gi]) / cnt[None, :, None])
    pooled = jnp.stack(outs, axis=2) - ug
    y = jnp.einsum('bsgc,gcd->bsgd', pooled.astype(u.dtype), pool_w)
    return y.reshape(B, S, POOL_WIDTH) * pool_scale


def compress(k, pos_emb, w1, b1, w2):
    B, S = k.shape[0], k.shape[1]
    n_cmp = (S - CMP_BLOCK) // CMP_STRIDE + 1
    idx = np.arange(n_cmp)[:, None] * CMP_STRIDE + np.arange(CMP_BLOCK)[None, :]
    blk = k[:, idx] + pos_emb[None, None, :, None, :]
    flat = jnp.moveaxis(blk, 3, 2).reshape(B, n_cmp, N_KV_GROUPS, CMP_BLOCK * HEAD_DIM)
    hid = jax.nn.gelu(flat @ w1 + b1)
    return hid @ w2


def nsa_attention(q, k, v, nsa_g, cmp_pos, cmp_w1, cmp_b1, cmp_w2):
    B, S = q.shape[0], q.shape[1]
    k_c, k_s, k_w = k[:, :, 0], k[:, :, 1], k[:, :, 2]
    v_c, v_s, v_w = v[:, :, 0], v[:, :, 1], v[:, :, 2]
    k_cmp = compress(k_c, cmp_pos[0], cmp_w1[0], cmp_b1[0], cmp_w2[0])
    v_cmp = compress(v_c, cmp_pos[1], cmp_w1[1], cmp_b1[1], cmp_w2[1])
    n_cmp = k_cmp.shape[1]
    cmp_end = jnp.asarray(np.arange(n_cmp) * CMP_STRIDE + CMP_BLOCK - 1, dtype=jnp.int32)
    n_slc = S // SLC_BLOCK
    n_sel = min(SLC_TOP_N, n_slc)
    ci = np.arange(n_cmp)[:, None]
    sj = np.arange(n_slc)[None, :]
    overlap = jnp.asarray(((ci * CMP_STRIDE < (sj + 1) * SLC_BLOCK) &
                           (ci * CMP_STRIDE + CMP_BLOCK > sj * SLC_BLOCK)).astype(np.float32))
    ks_blk = k_s.reshape(B, n_slc, SLC_BLOCK, N_KV_GROUPS, HEAD_DIM).transpose(0, 3, 1, 2, 4)
    vs_blk = v_s.reshape(B, n_slc, SLC_BLOCK, N_KV_GROUPS, HEAD_DIM).transpose(0, 3, 1, 2, 4)
    pad = ((0, 0), (WINDOW, 0), (0, 0), (0, 0))
    kw_pad = jnp.pad(k_w, pad)
    vw_pad = jnp.pad(v_w, pad)
    b_ix = jnp.arange(B)[:, None, None, None]
    g_ix = jnp.arange(N_KV_GROUPS)[None, :, None, None]
    blk_ids = jnp.arange(n_slc)

    def chunk(args):
        qc, gc, start = args
        tq = start + jnp.arange(Q_BLOCK)
        qg = qc.reshape(B, Q_BLOCK, N_KV_GROUPS, HEADS_PER_GROUP, HEAD_DIM)
        s_c = jnp.einsum('bqghd,bngd->bghqn', qg, k_cmp)
        p_c = masked_softmax(s_c, cmp_end[None, :] <= tq[:, None])
        o_c = jnp.einsum('bghqn,bngd->bqghd', p_c.astype(v_cmp.dtype), v_cmp)
        imp = jnp.einsum('bghqn,nj->bgqj', p_c, overlap)
        cur = tq // SLC_BLOCK
        forced = ((blk_ids[None] == 0) | (blk_ids[None] == cur[:, None]) |
                  (blk_ids[None] == cur[:, None] - 1)).astype(jnp.float32)
        valid_b = blk_ids[None] * SLC_BLOCK <= tq[:, None]
        score = jnp.where(valid_b, imp + FORCE_BONUS * forced, NEG_INF)
        _, idx = lax.top_k(score, n_sel)
        kg = ks_blk[b_ix, g_ix, idx]
        vg = vs_blk[b_ix, g_ix, idx]
        m_tok = n_sel * SLC_BLOCK
        kpos = idx[..., None] * SLC_BLOCK + jnp.arange(SLC_BLOCK)
        mask_s = (kpos <= tq[None, None, :, None, None]).reshape(B, N_KV_GROUPS, 1, Q_BLOCK, m_tok)
        s_s = jnp.einsum('bqghd,bgqnld->bghqnl', qg, kg).reshape(B, N_KV_GROUPS, HEADS_PER_GROUP, Q_BLOCK, m_tok)
        p_s = masked_softmax(s_s, mask_s)
        o_s = jnp.einsum('bghqm,bgqmd->bqghd', p_s.astype(vg.dtype),
                         vg.reshape(B, N_KV_GROUPS, Q_BLOCK, m_tok, HEAD_DIM))
        kw = lax.dynamic_slice_in_dim(kw_pad, start, Q_BLOCK + WINDOW, axis=1)
        vw = lax.dynamic_slice_in_dim(vw_pad, start, Q_BLOCK + WINDOW, axis=1)
        kp = start - WINDOW + jnp.arange(Q_BLOCK + WINDOW)
        dist = tq[:, None] - kp[None, :]
        mask_w = (dist >= 0) & (dist < WINDOW) & (kp[None, :] >= 0)
        s_w = jnp.einsum('bqghd,bkgd->bghqk', qg, kw)
        p_w = masked_softmax(s_w, mask_w)
        o_w = jnp.einsum('bghqk,bkgd->bqghd', p_w.astype(vw.dtype), vw)
        gc = gc.reshape(B, Q_BLOCK, N_KV_GROUPS, HEADS_PER_GROUP, N_NSA_BRANCHES)
        o = gc[..., 0:1] * o_c + gc[..., 1:2] * o_s + gc[..., 2:3] * o_w
        return o.reshape(B, Q_BLOCK, ATTN_WIDTH)

    n_q = S // Q_BLOCK
    q_chunks = q.reshape(B, n_q, Q_BLOCK, N_HEADS, HEAD_DIM).transpose(1, 0, 2, 3, 4)
    g_chunks = nsa_g.reshape(B, n_q, Q_BLOCK, N_HEADS, N_NSA_BRANCHES).transpose(1, 0, 2, 3, 4)
    starts = jnp.arange(n_q, dtype=jnp.int32) * Q_BLOCK
    out = lax.map(chunk, (q_chunks, g_chunks, starts))
    return out.transpose(1, 0, 2, 3).reshape(B, S, ATTN_WIDTH)


def token_mixer(h, w_in, pool_w, pool_scale, q_norm_w, k_norm_w, cmp_pos, cmp_w1, cmp_b1,
                cmp_w2, w_pool_br, w_attn_br, w_o):
    B, S, _ = h.shape
    z = h @ w_in
    u_pool = z[..., :OFF_Q]
    q = z[..., OFF_Q:OFF_KV].reshape(B, S, N_HEADS, HEAD_DIM)
    kv = z[..., OFF_KV:OFF_NSA_G].reshape(B, S, N_NSA_BRANCHES, 2, N_KV_GROUPS, HEAD_DIM)
    nsa_g = jax.nn.sigmoid(z[..., OFF_NSA_G:OFF_MERGE_G].astype(jnp.float32)).astype(h.dtype)
    nsa_g = nsa_g.reshape(B, S, N_HEADS, N_NSA_BRANCHES)
    merge_g = jax.nn.sigmoid(z[..., OFF_MERGE_G:].astype(jnp.float32)).astype(h.dtype)
    merge_g = merge_g.reshape(B, S, N_MERGE, D_MODEL)
    pos = jnp.arange(S)
    q = rope(rms_norm(q, q_norm_w), pos) * SCALE
    k = rope(rms_norm(kv[:, :, :, 0], k_norm_w[:, None, :]), pos)
    v = kv[:, :, :, 1]
    y_pool = pool_mixer(u_pool, pool_w, pool_scale) @ w_pool_br
    y_attn = nsa_attention(q, k, v, nsa_g, cmp_pos, cmp_w1, cmp_b1, cmp_w2) @ w_attn_br
    merged = merge_g[:, :, 0] * y_pool + merge_g[:, :, 1] * y_attn
    return merged @ w_o


def conv_ffn(h, w_up, conv_w, conv_b, w_down):
    S = h.shape[1]
    u = h @ w_up
    up = jnp.pad(u, ((0, 0), (CONV_WIDTH - 1, 0), (0, 0)))
    c = conv_b + conv_w[0] * up[:, 0:S]
    for tap in range(1, CONV_WIDTH):
        c = c + conv_w[tap] * up[:, tap:tap + S]
    gate, val = jnp.split(c, 2, axis=-1)
    return (jax.nn.silu(gate) * val) @ w_down


def setup_inputs(seed: int = 0) -> dict:
    key = jax.random.key(seed)
    ks = jax.random.split(key, 20)
    f32 = jnp.float32
    nrm = lambda k, shape, s: jax.random.normal(k, shape, f32) * s
    L = DEPTH
    return {
        "x": jax.random.normal(ks[0], (BATCH, SEQ, D_MODEL), f32),
        "attn_norm_w": 1.0 + nrm(ks[1], (L, D_MODEL), 0.05),
        "w_in": nrm(ks[2], (L, D_MODEL, IN_WIDTH), D_MODEL ** -0.5),
        "pool_w": nrm(ks[3], (L, N_POOL_GROUPS, POOL_GROUP, POOL_GROUP), POOL_GROUP ** -0.5),
        "pool_scale": 1.0 + nrm(ks[4], (L, POOL_WIDTH), 0.1),
        "q_norm_w": 1.0 + nrm(ks[5], (L, HEAD_DIM), 0.05),
        "k_norm_w": 1.0 + nrm(ks[6], (L, N_NSA_BRANCHES, HEAD_DIM), 0.05),
        "cmp_pos": nrm(ks[7], (L, 2, CMP_BLOCK, HEAD_DIM), 0.5),
        "cmp_w1": nrm(ks[8], (L, 2, CMP_BLOCK * HEAD_DIM, CMP_HIDDEN), (CMP_BLOCK * HEAD_DIM) ** -0.5),
        "cmp_b1": nrm(ks[9], (L, 2, CMP_HIDDEN), 0.02),
        "cmp_w2": nrm(ks[10], (L, 2, CMP_HIDDEN, HEAD_DIM), CMP_HIDDEN ** -0.5),
        "w_pool_br": nrm(ks[11], (L, POOL_WIDTH, D_MODEL), POOL_WIDTH ** -0.5),
        "w_attn_br": nrm(ks[12], (L, ATTN_WIDTH, D_MODEL), ATTN_WIDTH ** -0.5),
        "w_o": nrm(ks[13], (L, D_MODEL, D_MODEL), D_MODEL ** -0.5),
        "ffn_norm_w": 1.0 + nrm(ks[14], (L, D_MODEL), 0.05),
        "w_up": nrm(ks[15], (L, D_MODEL, 2 * D_FF), D_MODEL ** -0.5),
        "conv_w": nrm(ks[16], (L, CONV_WIDTH, 2 * D_FF), CONV_WIDTH ** -0.5),
        "conv_b": nrm(ks[17], (L, 2 * D_FF), 0.02),
        "w_down": nrm(ks[18], (L, D_FF, D_MODEL), D_FF ** -0.5),
    }


def reference(x, attn_norm_w, w_in, pool_w, pool_scale, q_norm_w, k_norm_w, cmp_pos, cmp_w1,
              cmp_b1, cmp_w2, w_pool_br, w_attn_br, w_o, ffn_norm_w, w_up, conv_w, conv_b, w_down):
    for l in range(DEPTH):
        h = rms_norm(x, attn_norm_w[l])
        x = x + token_mixer(h, w_in[l], pool_w[l], pool_scale[l], q_norm_w[l], k_norm_w[l],
                            cmp_pos[l], cmp_w1[l], cmp_b1[l], cmp_w2[l], w_pool_br[l],
                            w_attn_br[l], w_o[l])
        h = rms_norm(x, ffn_norm_w[l])
        x = x + conv_ffn(h, w_up[l], conv_w[l], conv_b[l], w_down[l])
    return x
```

```python
import functools

import numpy as np
import jax
import jax.numpy as jnp
from jax import lax
from jax.experimental import pallas as pl
from jax.experimental.pallas import tpu as pltpu

D_MODEL = 1024
POOL_WINDOWS = (2, 4, 8, 16)
N_POOL_GROUPS = 4
POOL_GROUP = 128
POOL_WIDTH = 512
HEAD_DIM = 64
N_HEADS = 16
N_KV_GROUPS = 2
HEADS_PER_GROUP = 8
ATTN_WIDTH = 1024
N_BRANCH = 3
KV_WIDTH = 768
CMP_BLOCK = 32
CMP_STRIDE = 16
CMP_HIDDEN = 128
SLC_BLOCK = 64
SLC_TOP_N = 8
WINDOW = 512
ROPE_THETA = 10000.0
SCALE = HEAD_DIM ** -0.5
FORCE_BONUS = 1000.0
NEG_INF = -1e30
OFF_Q = POOL_WIDTH
OFF_KV = OFF_Q + ATTN_WIDTH
OFF_NSA_G = OFF_KV + KV_WIDTH
OFF_MERGE_G = OFF_NSA_G + N_BRANCH * N_HEADS
D_FF = 2752
CONV_WIDTH = 3
EPS = 1e-6

LANES = 128
MXU_N = 256
VMEM_LIMIT = 56 * 1024 * 1024

TM = 512
TQ = 256
D_FF_PAD = 2816
FF_CHUNK = 256
N_FF_CHUNKS = D_FF_PAD // FF_CHUNK
POOL_HALO = 16
CONV_HALO = 8
N_CMP_PAD = 128
N_SLC_PAD = 128

F32 = jnp.float32
BF16 = jnp.bfloat16


def _dot(a, b):
    return jnp.dot(a, b, preferred_element_type=F32)


def _dot_nt(a, b):
    return lax.dot_general(a, b, (((1,), (1,)), ((), ())), preferred_element_type=F32)


def _split_dot(a, b):
    hi = a.astype(BF16)
    lo = (a - hi.astype(F32)).astype(BF16)
    return _dot(hi, b) + _dot(lo, b)


_SLAB_POOL = 0
_SLAB_Q = 4
_SLAB_K = 12
_SLAB_V = 15
_SLAB_MERGE = 18
_SLAB_GATE = 34
_N_SLABS = 35


def _inproj_kernel(x_ref, anw_ref, w_ref, nw_ref, cos_ref, sa_ref, sb_ref,
                   up_ref, q_ref, kc_ref, ks_ref, kw_ref, vc_ref, vs_ref, vw_ref, mg_ref, g_ref):
    x = x_ref[...]
    ms = jnp.mean(x * x, axis=-1, keepdims=True)
    h = ((x * lax.rsqrt(ms + EPS)) * anw_ref[...]).astype(BF16)
    tm = x.shape[0]
    lane = lax.broadcasted_iota(jnp.int32, (tm, LANES), 1)
    lo = lane < HEAD_DIM
    cos = cos_ref[...]
    sa = sa_ref[...]
    sb = sb_ref[...]

    def norm_rope(z, nw, scale):
        ss = z * z
        s_lo = jnp.sum(jnp.where(lo, ss, 0.0), axis=-1, keepdims=True)
        s_hi = jnp.sum(jnp.where(lo, 0.0, ss), axis=-1, keepdims=True)
        msq = jnp.where(lo, s_lo, s_hi) * (1.0 / HEAD_DIM)
        y = (z * lax.rsqrt(msq + EPS)) * nw
        r = y * cos + pltpu.roll(y, LANES - HEAD_DIM // 2, 1) * sa + pltpu.roll(y, HEAD_DIM // 2, 1) * sb
        return r * scale if scale is not None else r

    k_refs = (kc_ref, ks_ref, kw_ref)
    v_refs = (vc_ref, vs_ref, vw_ref)

    def emit(slab, z):
        if slab < _SLAB_Q:
            c = (slab - _SLAB_POOL) * LANES
            up_ref[:, c:c + LANES] = z
        elif slab < _SLAB_K:
            c = (slab - _SLAB_Q) * LANES
            q_ref[:, c:c + LANES] = norm_rope(z, nw_ref[0:1, :], SCALE).astype(BF16)
        elif slab < _SLAB_V:
            br = slab - _SLAB_K
            r = norm_rope(z, nw_ref[1 + br:2 + br, :], None)
            k_refs[br][...] = r.astype(k_refs[br].dtype)
        elif slab < _SLAB_MERGE:
            br = slab - _SLAB_V
            v_refs[br][...] = z.astype(v_refs[br].dtype)
        elif slab < _SLAB_GATE:
            c = (slab - _SLAB_MERGE) * LANES
            mg_ref[:, c:c + LANES] = jax.nn.sigmoid(z).astype(BF16)
        else:
            g_ref[...] = jax.nn.sigmoid(z)

    slab = 0
    while slab < _N_SLABS:
        width = 2 if slab + 1 < _N_SLABS else 1
        z = _dot(h, w_ref[:, slab * LANES:(slab + width) * LANES])
        for j in range(width):
            emit(slab + j, z[:, j * LANES:(j + 1) * LANES])
        slab += width


def _inproj(x2, anw, w_perm, normw, cos_t, sin_a, sin_b, seq):
    t = x2.shape[0]
    tiles_per_seq = seq // TM
    row = lambda i: (i, 0)
    const = lambda i: (0, 0)
    pos = lambda i: (i % tiles_per_seq, 0)
    out_shapes = (
        jax.ShapeDtypeStruct((t, POOL_WIDTH), F32),
        jax.ShapeDtypeStruct((t, ATTN_WIDTH), BF16),
        jax.ShapeDtypeStruct((t, LANES), F32),
        jax.ShapeDtypeStruct((t, LANES), BF16),
        jax.ShapeDtypeStruct((t, LANES), BF16),
        jax.ShapeDtypeStruct((t, LANES), F32),
        jax.ShapeDtypeStruct((t, LANES), BF16),
        jax.ShapeDtypeStruct((t, LANES), BF16),
        jax.ShapeDtypeStruct((t, 2 * D_MODEL), BF16),
        jax.ShapeDtypeStruct((t, LANES), F32),
    )
    out_specs = tuple(pl.BlockSpec((TM, s.shape[1]), row) for s in out_shapes)
    return pl.pallas_call(
        _inproj_kernel,
        grid=(t // TM,),
        in_specs=[
            pl.BlockSpec((TM, D_MODEL), row),
            pl.BlockSpec((1, D_MODEL), const),
            pl.BlockSpec((D_MODEL, _N_SLABS * LANES), const),
            pl.BlockSpec((4, LANES), const),
            pl.BlockSpec((TM, LANES), pos),
            pl.BlockSpec((TM, LANES), pos),
            pl.BlockSpec((TM, LANES), pos),
        ],
        out_specs=out_specs,
        out_shape=out_shapes,
        compiler_params=pltpu.CompilerParams(
            dimension_semantics=("arbitrary",), vmem_limit_bytes=VMEM_LIMIT),
        name="inproj",
    )(x2, anw, w_perm, normw, cos_t, sin_a, sin_b)


def _compress_kernel(x_ref, pos_ref, w1_ref, b1_ref, w2_ref, o_ref):
    x = x_ref[0, 0, 0]
    half = CMP_STRIDE * HEAD_DIM
    xt = (x + pos_ref[0, 0:1, :]).astype(BF16)
    xb = (x + pos_ref[0, 1:2, :]).astype(BF16)
    a = _dot(xt, w1_ref[0, 0:half, :])
    b = _dot(xb, w1_ref[0, half:2 * half, :])
    n = x.shape[0]
    hid = a + pltpu.roll(b, n - 1, 0) + b1_ref[0]
    act = jax.nn.gelu(hid)
    o_ref[0, 0, 0] = _dot(act.astype(BF16), w2_ref[0]).astype(o_ref.dtype)


def _compress(x16, pos2, w1, b1, w2d):
    _, b, g, n, width = x16.shape
    return pl.pallas_call(
        _compress_kernel,
        grid=(2, b, g),
        in_specs=[
            pl.BlockSpec((1, 1, 1, n, width), lambda a, i, j: (a, i, j, 0, 0)),
            pl.BlockSpec((1, 2, width), lambda a, i, j: (a, 0, 0)),
            pl.BlockSpec((1, 2 * width, CMP_HIDDEN), lambda a, i, j: (a, 0, 0)),
            pl.BlockSpec((1, 1, CMP_HIDDEN), lambda a, i, j: (a, 0, 0)),
            pl.BlockSpec((1, CMP_HIDDEN, LANES), lambda a, i, j: (a, 0, 0)),
        ],
        out_specs=pl.BlockSpec((1, 1, 1, n, LANES), lambda a, i, j: (a, i, j, 0, 0)),
        out_shape=jax.ShapeDtypeStruct((2, b, g, n, LANES), BF16),
        compiler_params=pltpu.CompilerParams(
            dimension_semantics=("arbitrary", "arbitrary", "arbitrary"), vmem_limit_bytes=VMEM_LIMIT),
        name="compress",
    )(x16, pos2, w1, b1, w2d)


def _attn_kernel(q_ref, ks_ref, kw_ref, vs_ref, vw_ref, ck_ref, cv_ref, gate_ref, e_ref, ov_ref, eb_ref,
                 o_ref, kks, kkw, vvs, vvw, qst, bias_ref, m_ref, l_ref, acc_ref):
    g = pl.program_id(1)
    i = pl.program_id(2)
    seq = ks_ref.shape[0]
    nk = seq // TQ
    hq = HEADS_PER_GROUP * TQ

    @pl.when(i == 0)
    def _prep():
        lane = lax.broadcasted_iota(jnp.int32, (seq, LANES), 1)
        mine = (lane >= HEAD_DIM).astype(jnp.int32) == g
        for src, dst in ((ks_ref, kks), (kw_ref, kkw), (vs_ref, vvs), (vw_ref, vvw)):
            x = src[...]
            xr = jnp.concatenate([x[:, HEAD_DIM:], x[:, :HEAD_DIM]], axis=1)
            dst[...] = jnp.where(mine, x, xr)

    lane_q = lax.broadcasted_iota(jnp.int32, (TQ, LANES), 1)
    lo_q = lane_q < HEAD_DIM
    for p in range(HEADS_PER_GROUP // 2):
        qp = q_ref[:, p * LANES:(p + 1) * LANES]
        zero = jnp.zeros_like(qp)
        qst[(2 * p) * TQ:(2 * p + 1) * TQ, :] = jnp.where(lo_q, qp, zero)
        qst[(2 * p + 1) * TQ:(2 * p + 2) * TQ, :] = jnp.where(lo_q, zero, qp)

    row = lax.broadcasted_iota(jnp.int32, (TQ, LANES), 0)
    tq = i * TQ + row

    sc = _dot_nt(qst[...], ck_ref[0, 0, 0]).reshape(HEADS_PER_GROUP, TQ, N_CMP_PAD)
    cmp_mask = (lane_q * CMP_STRIDE + (CMP_BLOCK - 1)) <= tq
    s = jnp.where(cmp_mask[None], sc, NEG_INF)
    m = jnp.max(s, axis=-1, keepdims=True)
    e = jnp.where(cmp_mask[None], jnp.exp(s - m), 0.0)
    l = jnp.sum(e, axis=-1, keepdims=True)
    pc = e * jnp.where(l > 0.0, 1.0 / l, 0.0)
    o_c = _dot(pc.reshape(hq, N_CMP_PAD).astype(BF16), cv_ref[0, 0, 0])

    imp = _split_dot(jnp.sum(pc, axis=0), ov_ref[...])
    cur = tq // SLC_BLOCK
    forced = (lane_q == 0) | (lane_q == cur) | (lane_q == cur - 1)
    valid = lane_q * SLC_BLOCK <= tq
    score = jnp.where(valid, imp + jnp.where(forced, FORCE_BONUS, 0.0), NEG_INF)
    score = jnp.where(lane_q < seq // SLC_BLOCK, score, -jnp.inf)
    sel = jnp.zeros((TQ, LANES), F32)
    lane_f = lane_q.astype(F32)
    for _ in range(SLC_TOP_N):
        mx = jnp.max(score, axis=-1, keepdims=True)
        first = jnp.min(jnp.where(score == mx, lane_f, float(LANES)), axis=-1, keepdims=True)
        pick = lane_f == first
        sel = jnp.where(pick, 1.0, sel)
        score = jnp.where(pick, -jnp.inf, score)
    key_sel = _dot(sel.astype(BF16), eb_ref[...])
    for kt in range(nk):
        bias_ref[kt] = (key_sel[:, kt * TQ:(kt + 1) * TQ] - 1.0) * (-NEG_INF)

    col = lax.broadcasted_iota(jnp.int32, (TQ, TQ), 1)
    rowk = lax.broadcasted_iota(jnp.int32, (TQ, TQ), 0)
    causal_bias = jnp.where(col <= rowk, 0.0, NEG_INF)
    tail_bias = jnp.where(col > rowk, 0.0, NEG_INF)

    def init_state():
        m_ref[...] = jnp.full((hq, 1), 2.0 * NEG_INF, F32)
        l_ref[...] = jnp.zeros((hq, 1), F32)
        acc_ref[...] = jnp.zeros((hq, LANES), F32)

    def step(k_tile, v_tile, bias):
        s = _dot_nt(qst[...], k_tile)
        if bias is not None:
            s = (s.reshape(HEADS_PER_GROUP, TQ, TQ) + bias[None]).reshape(hq, TQ)
        m_old = m_ref[...]
        m_new = jnp.maximum(m_old, jnp.max(s, axis=-1, keepdims=True))
        alpha = jnp.exp(m_old - m_new)
        p = jnp.exp(s - m_new)
        l_ref[...] = alpha * l_ref[...] + jnp.sum(p, axis=-1, keepdims=True)
        acc_ref[...] = alpha * acc_ref[...] + _dot(p.astype(BF16), v_tile)
        m_ref[...] = m_new

    def result():
        return acc_ref[...] * (1.0 / l_ref[...])

    def tile(ref, kt):
        return ref[pl.ds(pl.multiple_of(kt * TQ, TQ), TQ), :]

    init_state()

    def sel_body(kt, carry):
        step(tile(kks, kt), tile(vvs, kt), bias_ref[kt])
        return carry

    lax.fori_loop(0, i, sel_body, 0)
    step(tile(kks, i), tile(vvs, i), bias_ref[i] + causal_bias)
    o_s = result()

    init_state()

    @pl.when(i >= 2)
    def _w2():
        step(tile(kkw, i - 2), tile(vvw, i - 2), tail_bias)

    @pl.when(i >= 1)
    def _w1():
        step(tile(kkw, i - 1), tile(vvw, i - 1), None)

    step(tile(kkw, i), tile(vvw, i), causal_bias)
    o_w = result()

    gate = gate_ref[...]
    out = None
    for br, o_b in enumerate((o_c, o_s, o_w)):
        g_exp = _split_dot(gate, e_ref[0, br])
        pairs = [jnp.where(lo_q, o_b[(2 * p) * TQ:(2 * p + 1) * TQ], o_b[(2 * p + 1) * TQ:(2 * p + 2) * TQ])
                 for p in range(HEADS_PER_GROUP // 2)]
        term = g_exp * jnp.concatenate(pairs, axis=1)
        out = term if out is None else out + term
    o_ref[...] = out.astype(o_ref.dtype)


def _attention(q, ks, kw, vs, vw, cmp_kv, gate, e_mat, ov_mat, eb_mat, batch, seq):
    nq = seq // TQ
    gw = HEADS_PER_GROUP * HEAD_DIM
    hq = HEADS_PER_GROUP * TQ
    kv_spec = pl.BlockSpec((seq, LANES), lambda b, g, i: (b, 0))
    return pl.pallas_call(
        _attn_kernel,
        grid=(batch, N_KV_GROUPS, nq),
        in_specs=[
            pl.BlockSpec((TQ, gw), lambda b, g, i: (b * nq + i, g)),
            kv_spec, kv_spec, kv_spec, kv_spec,
            pl.BlockSpec((1, 1, 1, N_CMP_PAD, LANES), lambda b, g, i: (0, b, g, 0, 0)),
            pl.BlockSpec((1, 1, 1, N_CMP_PAD, LANES), lambda b, g, i: (1, b, g, 0, 0)),
            pl.BlockSpec((TQ, LANES), lambda b, g, i: (b * nq + i, 0)),
            pl.BlockSpec((1, N_BRANCH, LANES, gw), lambda b, g, i: (g, 0, 0, 0)),
            pl.BlockSpec((N_CMP_PAD, N_SLC_PAD), lambda b, g, i: (0, 0)),
            pl.BlockSpec((N_SLC_PAD, seq), lambda b, g, i: (0, 0)),
        ],
        out_specs=pl.BlockSpec((TQ, gw), lambda b, g, i: (b * nq + i, g)),
        out_shape=jax.ShapeDtypeStruct((batch * seq, ATTN_WIDTH), BF16),
        scratch_shapes=[
            pltpu.VMEM((seq, LANES), BF16), pltpu.VMEM((seq, LANES), BF16),
            pltpu.VMEM((seq, LANES), BF16), pltpu.VMEM((seq, LANES), BF16),
            pltpu.VMEM((hq, LANES), BF16),
            pltpu.VMEM((seq // TQ, TQ, TQ), F32),
            pltpu.VMEM((hq, 1), F32), pltpu.VMEM((hq, 1), F32), pltpu.VMEM((hq, LANES), F32),
        ],
        compiler_params=pltpu.CompilerParams(
            dimension_semantics=("arbitrary", "arbitrary", "arbitrary"), vmem_limit_bytes=VMEM_LIMIT),
        name="nsa_attention",
    )(q, ks, kw, vs, vw, cmp_kv, cmp_kv, gate, e_mat, ov_mat, eb_mat)


def _merge_kernel(x_ref, up_ref, at_ref, mg_ref, pw_ref, ps_ref, wpb_ref, wab_ref, wo_ref,
                  o_ref, halo_ref, *, tiles_per_seq):
    i = pl.program_id(0)
    tm = x_ref.shape[0]
    u = up_ref[...]

    @pl.when((i % tiles_per_seq) == 0)
    def _seq_start():
        halo_ref[...] = jnp.zeros_like(halo_ref)

    halo = halo_ref[...]
    halo_ref[...] = u[tm - POOL_HALO:, :]
    ue = jnp.concatenate([halo, u], axis=0)
    rowp = lax.broadcasted_iota(jnp.int32, (tm, POOL_GROUP), 0) + (i % tiles_per_seq) * tm
    ys = []
    for gi, w in enumerate(POOL_WINDOWS):
        xg = ue[:, gi * POOL_GROUP:(gi + 1) * POOL_GROUP]
        sw = xg
        k = 1
        while k < w:
            sw = sw + pltpu.roll(sw, k, 0)
            k *= 2
        cnt = jnp.minimum(rowp + 1, w).astype(F32)
        ug = u[:, gi * POOL_GROUP:(gi + 1) * POOL_GROUP]
        pooled = sw[POOL_HALO:, :] / cnt - ug
        ys.append(_dot(pooled.astype(BF16), pw_ref[gi]))
    y = jnp.concatenate(ys, axis=1) * ps_ref[...]
    y_pool = _dot(y.astype(BF16), wpb_ref[...])
    y_attn = _dot(at_ref[...], wab_ref[...])
    mg = mg_ref[...]
    merged = mg[:, :D_MODEL] * y_pool + mg[:, D_MODEL:] * y_attn
    o_ref[...] = x_ref[...] + _dot(merged.astype(BF16), wo_ref[...])


def _merge(x2, u_pool, attn, mg, pool_w, pool_scale, w_pool_br, w_attn_br, w_o, seq):
    t = x2.shape[0]
    row = lambda i: (i, 0)
    const = lambda i: (0, 0)
    return pl.pallas_call(
        functools.partial(_merge_kernel, tiles_per_seq=seq // TM),
        grid=(t // TM,),
        in_specs=[
            pl.BlockSpec((TM, D_MODEL), row),
            pl.BlockSpec((TM, POOL_WIDTH), row),
            pl.BlockSpec((TM, ATTN_WIDTH), row),
            pl.BlockSpec((TM, 2 * D_MODEL), row),
            pl.BlockSpec((N_POOL_GROUPS, POOL_GROUP, POOL_GROUP), lambda i: (0, 0, 0)),
            pl.BlockSpec((1, POOL_WIDTH), const),
            pl.BlockSpec((POOL_WIDTH, D_MODEL), const),
            pl.BlockSpec((ATTN_WIDTH, D_MODEL), const),
            pl.BlockSpec((D_MODEL, D_MODEL), const),
        ],
        out_specs=pl.BlockSpec((TM, D_MODEL), row),
        out_shape=jax.ShapeDtypeStruct((t, D_MODEL), F32),
        scratch_shapes=[pltpu.VMEM((POOL_HALO, POOL_WIDTH), F32)],
        compiler_params=pltpu.CompilerParams(
            dimension_semantics=("arbitrary",), vmem_limit_bytes=VMEM_LIMIT),
        name="pool_merge",
    )(x2, u_pool, attn, mg, pool_w, pool_scale, w_pool_br, w_attn_br, w_o)


def _ffn_kernel(x_ref, nw_ref, wup_ref, cw_ref, cb_ref, wdn_ref, o_ref, halo_ref, acc_ref, *, tiles_per_seq):
    i = pl.program_id(0)
    tm = x_ref.shape[0]
    x = x_ref[...]
    ms = jnp.mean(x * x, axis=-1, keepdims=True)
    h = ((x * lax.rsqrt(ms + EPS)) * nw_ref[...]).astype(BF16)

    @pl.when((i % tiles_per_seq) == 0)
    def _seq_start():
        halo_ref[...] = jnp.zeros_like(halo_ref)

    def conv(c):
        u = _dot(h, wup_ref[c])
        halo = halo_ref[c]
        halo_ref[c] = u[tm - CONV_HALO:, :]
        ue = jnp.concatenate([halo, u], axis=0)
        w = cw_ref[c]
        u1 = pltpu.roll(ue, 1, 0)[CONV_HALO:, :]
        u2 = pltpu.roll(ue, 2, 0)[CONV_HALO:, :]
        return cb_ref[c] + w[0:1, :] * u2 + w[1:2, :] * u1 + w[2:3, :] * u

    for c in range(N_FF_CHUNKS):
        gate = conv(c)
        val = conv(N_FF_CHUNKS + c)
        act = (gate * jax.nn.sigmoid(gate)) * val
        part = _dot(act.astype(BF16), wdn_ref[c])
        if c == 0:
            acc_ref[...] = part
        else:
            acc_ref[...] += part
    o_ref[...] = x + acc_ref[...]


def _ffn(x1, nw, wup, cw, cb, wdn, seq):
    t = x1.shape[0]
    row = lambda i: (i, 0)
    c3 = lambda i: (0, 0, 0)
    return pl.pallas_call(
        functools.partial(_ffn_kernel, tiles_per_seq=seq // TM),
        grid=(t // TM,),
        in_specs=[
            pl.BlockSpec((TM, D_MODEL), row),
            pl.BlockSpec((1, D_MODEL), lambda i: (0, 0)),
            pl.BlockSpec((2 * N_FF_CHUNKS, D_MODEL, FF_CHUNK), c3),
            pl.BlockSpec((2 * N_FF_CHUNKS, CONV_HALO, FF_CHUNK), c3),
            pl.BlockSpec((2 * N_FF_CHUNKS, 1, FF_CHUNK), c3),
            pl.BlockSpec((N_FF_CHUNKS, FF_CHUNK, D_MODEL), c3),
        ],
        out_specs=pl.BlockSpec((TM, D_MODEL), row),
        out_shape=jax.ShapeDtypeStruct((t, D_MODEL), F32),
        scratch_shapes=[pltpu.VMEM((2 * N_FF_CHUNKS, CONV_HALO, FF_CHUNK), F32),
                        pltpu.VMEM((TM, D_MODEL), F32)],
        compiler_params=pltpu.CompilerParams(
            dimension_semantics=("arbitrary",), vmem_limit_bytes=VMEM_LIMIT),
        name="conv_ffn",
    )(x1, nw, wup, cw, cb, wdn)


def _w_in_columns():
    cols = list(range(0, OFF_KV))
    for kv in range(2):
        for br in range(N_BRANCH):
            base = OFF_KV + br * 4 * HEAD_DIM + kv * 2 * HEAD_DIM
            cols += list(range(base, base + 2 * HEAD_DIM))
    cols += list(range(OFF_MERGE_G, OFF_MERGE_G + 2 * D_MODEL))
    cols += [OFF_NSA_G + h * N_BRANCH + br for br in range(N_BRANCH) for h in range(N_HEADS)]
    return np.asarray(cols, np.int32)


def _rope_tables(seq):
    half = HEAD_DIM // 2
    freqs = ROPE_THETA ** (-jnp.arange(half, dtype=F32) / half)
    ang = jnp.arange(seq, dtype=F32)[:, None] * freqs[None, :]
    cos = jnp.cos(ang)
    sin = jnp.sin(ang)
    zero = jnp.zeros_like(sin)
    cos_t = jnp.tile(cos, (1, LANES // half))
    sin_a = jnp.tile(jnp.concatenate([-sin, zero], axis=1), (1, LANES // HEAD_DIM))
    sin_b = jnp.tile(jnp.concatenate([zero, sin], axis=1), (1, LANES // HEAD_DIM))
    return cos_t, sin_a, sin_b


def _selection_constants(seq):
    n_cmp = (seq - CMP_BLOCK) // CMP_STRIDE + 1
    n_slc = seq // SLC_BLOCK
    ci = np.arange(N_CMP_PAD)[:, None]
    sj = np.arange(N_SLC_PAD)[None, :]
    ov = ((ci * CMP_STRIDE < (sj + 1) * SLC_BLOCK) & (ci * CMP_STRIDE + CMP_BLOCK > sj * SLC_BLOCK)
          & (ci < n_cmp) & (sj < n_slc))
    eb = (np.arange(seq)[None, :] // SLC_BLOCK) == np.arange(N_SLC_PAD)[:, None]
    gw = HEADS_PER_GROUP * HEAD_DIM
    e = np.zeros((N_KV_GROUPS, N_BRANCH, LANES, gw), np.float32)
    for g in range(N_KV_GROUPS):
        for br in range(N_BRANCH):
            for h in range(HEADS_PER_GROUP):
                e[g, br, br * N_HEADS + g * HEADS_PER_GROUP + h, h * HEAD_DIM:(h + 1) * HEAD_DIM] = 1.0
    return (jnp.asarray(ov, BF16), jnp.asarray(eb, BF16), jnp.asarray(e, BF16))


def _chunk_cols(w, n_chunks):
    return jnp.moveaxis(w.reshape(w.shape[:-1] + (n_chunks, FF_CHUNK)), -2, 0)


def _layer(x, attn_norm_w, w_in, pool_w, pool_scale, q_norm_w, k_norm_w, cmp_pos, cmp_w1, cmp_b1, cmp_w2,
           w_pool_br, w_attn_br, w_o, ffn_norm_w, w_up, conv_w, conv_b, w_down):
    batch, seq, _ = x.shape
    t = batch * seq
    x2 = x.reshape(t, D_MODEL)

    w_perm = jnp.take(w_in, _w_in_columns(), axis=1)
    w_perm = jnp.pad(w_perm, ((0, 0), (0, _N_SLABS * LANES - w_perm.shape[1]))).astype(BF16)
    normw = jnp.concatenate([jnp.tile(q_norm_w[None, :], (1, 2)), jnp.tile(k_norm_w, (1, 2))], axis=0)
    cos_t, sin_a, sin_b = _rope_tables(seq)
    u_pool, q, kc, ks, kw, vc, vs, vw, mg, gate = _inproj(
        x2, attn_norm_w[None, :], w_perm, normw, cos_t, sin_a, sin_b, seq)

    def rows16(a):
        a = a.reshape(batch, seq // CMP_STRIDE, CMP_STRIDE, N_KV_GROUPS, HEAD_DIM)
        return a.transpose(0, 3, 1, 2, 4).reshape(batch, N_KV_GROUPS, seq // CMP_STRIDE, CMP_STRIDE * HEAD_DIM)

    x16 = jnp.stack([rows16(kc), rows16(vc)], axis=0)
    pos2 = cmp_pos.reshape(2, 2, CMP_STRIDE * HEAD_DIM)
    w2d = jnp.concatenate([cmp_w2, cmp_w2], axis=-1).astype(BF16)
    cmp_kv = _compress(x16, pos2, cmp_w1.astype(BF16), cmp_b1[:, None, :], w2d)

    ov_mat, eb_mat, e_mat = _selection_constants(seq)
    attn = _attention(q, ks, kw, vs, vw, cmp_kv, gate, e_mat, ov_mat, eb_mat, batch, seq)

    x1 = _merge(x2, u_pool, attn, mg, pool_w.astype(BF16), pool_scale[None, :], w_pool_br.astype(BF16),
                w_attn_br.astype(BF16), w_o.astype(BF16), seq)

    pad = D_FF_PAD - D_FF

    def pad_halves(w):
        lead = [(0, 0)] * (w.ndim - 1)
        return jnp.concatenate([jnp.pad(w[..., :D_FF], lead + [(0, pad)]),
                                jnp.pad(w[..., D_FF:], lead + [(0, pad)])], axis=-1)

    wup = _chunk_cols(pad_halves(w_up).astype(BF16), 2 * N_FF_CHUNKS)
    cw = _chunk_cols(jnp.pad(pad_halves(conv_w), ((0, CONV_HALO - CONV_WIDTH), (0, 0))), 2 * N_FF_CHUNKS)
    cb = _chunk_cols(pad_halves(conv_b)[None, :], 2 * N_FF_CHUNKS)
    wdn = jnp.pad(w_down, ((0, pad), (0, 0))).astype(BF16).reshape(N_FF_CHUNKS, FF_CHUNK, D_MODEL)
    out = _ffn(x1, ffn_norm_w[None, :], wup, cw, cb, wdn, seq)
    return out.reshape(batch, seq, D_MODEL)


def kernel(x, attn_norm_w, w_in, pool_w, pool_scale, q_norm_w, k_norm_w, cmp_pos, cmp_w1, cmp_b1, cmp_w2,
           w_pool_br, w_attn_br, w_o, ffn_norm_w, w_up, conv_w, conv_b, w_down):
    for l in range(attn_norm_w.shape[0]):
        x = _layer(x, attn_norm_w[l], w_in[l], pool_w[l], pool_scale[l], q_norm_w[l], k_norm_w[l],
                   cmp_pos[l], cmp_w1[l], cmp_b1[l], cmp_w2[l], w_pool_br[l], w_attn_br[l], w_o[l],
                   ffn_norm_w[l], w_up[l], conv_w[l], conv_b[l], w_down[l])
    return x
```

```python
import functools

import numpy as np
import jax
import jax.numpy as jnp
from jax import lax
from jax.experimental import pallas as pl
from jax.experimental.pallas import tpu as pltpu

D_MODEL = 1024
POOL_WINDOWS = (2, 4, 8, 16)
N_POOL_GROUPS = 4
POOL_GROUP = 128
POOL_WIDTH = 512
HEAD_DIM = 64
N_HEADS = 16
N_KV_GROUPS = 2
HEADS_PER_GROUP = 8
ATTN_WIDTH = 1024
N_BRANCH = 3
KV_WIDTH = 768
CMP_BLOCK = 32
CMP_STRIDE = 16
CMP_HIDDEN = 128
SLC_BLOCK = 64
SLC_TOP_N = 8
WINDOW = 512
ROPE_THETA = 10000.0
SCALE = HEAD_DIM ** -0.5
FORCE_BONUS = 1000.0
NEG_INF = -1e30
OFF_Q = POOL_WIDTH
OFF_KV = OFF_Q + ATTN_WIDTH
OFF_NSA_G = OFF_KV + KV_WIDTH
OFF_MERGE_G = OFF_NSA_G + N_BRANCH * N_HEADS
D_FF = 2752
CONV_WIDTH = 3
EPS = 1e-6
LOG2E = 1.4426950408889634

LANES = 128
MXU_N = 256
VMEM_LIMIT = 56 * 1024 * 1024

TM = 512
TQ = 256
D_FF_PAD = 2816
FF_CHUNK = 256
N_FF_CHUNKS = D_FF_PAD // FF_CHUNK
POOL_HALO = 16
CONV_HALO = 8
N_CMP_PAD = 128
N_SLC_PAD = 128

F32 = jnp.float32
BF16 = jnp.bfloat16


def _dot(a, b):
    return jnp.dot(a, b, preferred_element_type=F32)


def _dot_nt(a, b):
    return lax.dot_general(a, b, (((1,), (1,)), ((), ())), preferred_element_type=F32)


def _split_dot(a, b):
    hi = a.astype(BF16)
    lo = (a - hi.astype(F32)).astype(BF16)
    return _dot(hi, b) + _dot(lo, b)


_SLAB_POOL = 0
_SLAB_Q = 4
_SLAB_K = 12
_SLAB_V = 15
_SLAB_MERGE = 18
_SLAB_GATE = 34
_N_SLABS = 35


def _inproj_kernel(x_ref, anw_ref, w_ref, nw_ref, cos_ref, sa_ref, sb_ref,
                   up_ref, q_ref, kc_ref, ks_ref, kw_ref, vc_ref, vs_ref, vw_ref, mg_ref, g_ref):
    x = x_ref[...]
    ms = jnp.mean(x * x, axis=-1, keepdims=True)
    h = ((x * lax.rsqrt(ms + EPS)) * anw_ref[...]).astype(BF16)
    tm = x.shape[0]
    lane = lax.broadcasted_iota(jnp.int32, (tm, LANES), 1)
    lo = lane < HEAD_DIM
    cos = cos_ref[...]
    sa = sa_ref[...]
    sb = sb_ref[...]

    def norm_rope(z, nw, scale):
        ss = z * z
        s_lo = jnp.sum(jnp.where(lo, ss, 0.0), axis=-1, keepdims=True)
        s_hi = jnp.sum(jnp.where(lo, 0.0, ss), axis=-1, keepdims=True)
        msq = jnp.where(lo, s_lo, s_hi) * (1.0 / HEAD_DIM)
        y = (z * lax.rsqrt(msq + EPS)) * nw
        r = y * cos + pltpu.roll(y, LANES - HEAD_DIM // 2, 1) * sa + pltpu.roll(y, HEAD_DIM // 2, 1) * sb
        return r * scale if scale is not None else r

    k_refs = (kc_ref, ks_ref, kw_ref)
    v_refs = (vc_ref, vs_ref, vw_ref)

    def emit(slab, z):
        if slab < _SLAB_Q:
            c = (slab - _SLAB_POOL) * LANES
            up_ref[:, c:c + LANES] = z
        elif slab < _SLAB_K:
            c = (slab - _SLAB_Q) * LANES
            q_ref[:, c:c + LANES] = norm_rope(z, nw_ref[0:1, :], SCALE * LOG2E).astype(BF16)
        elif slab < _SLAB_V:
            br = slab - _SLAB_K
            r = norm_rope(z, nw_ref[1 + br:2 + br, :], None)
            k_refs[br][...] = r.astype(k_refs[br].dtype)
        elif slab < _SLAB_MERGE:
            br = slab - _SLAB_V
            v_refs[br][...] = z.astype(v_refs[br].dtype)
        elif slab < _SLAB_GATE:
            c = (slab - _SLAB_MERGE) * LANES
            mg_ref[:, c:c + LANES] = jax.nn.sigmoid(z).astype(BF16)
        else:
            g_ref[...] = jax.nn.sigmoid(z)

    slab = 0
    while slab < _N_SLABS:
        width = 2 if slab + 1 < _N_SLABS else 1
        z = _dot(h, w_ref[:, slab * LANES:(slab + width) * LANES])
        for j in range(width):
            emit(slab + j, z[:, j * LANES:(j + 1) * LANES])
        slab += width


def _inproj(x2, anw, w_perm, normw, cos_t, sin_a, sin_b, seq):
    t = x2.shape[0]
    tiles_per_seq = seq // TM
    row = lambda i: (i, 0)
    const = lambda i: (0, 0)
    pos = lambda i: (i % tiles_per_seq, 0)
    out_shapes = (
        jax.ShapeDtypeStruct((t, POOL_WIDTH), F32),
        jax.ShapeDtypeStruct((t, ATTN_WIDTH), BF16),
        jax.ShapeDtypeStruct((t, LANES), F32),
        jax.ShapeDtypeStruct((t, LANES), BF16),
        jax.ShapeDtypeStruct((t, LANES), BF16),
        jax.ShapeDtypeStruct((t, LANES), F32),
        jax.ShapeDtypeStruct((t, LANES), BF16),
        jax.ShapeDtypeStruct((t, LANES), BF16),
        jax.ShapeDtypeStruct((t, 2 * D_MODEL), BF16),
        jax.ShapeDtypeStruct((t, LANES), F32),
    )
    out_specs = tuple(pl.BlockSpec((TM, s.shape[1]), row) for s in out_shapes)
    return pl.pallas_call(
        _inproj_kernel,
        grid=(t // TM,),
        in_specs=[
            pl.BlockSpec((TM, D_MODEL), row),
            pl.BlockSpec((1, D_MODEL), const),
            pl.BlockSpec((D_MODEL, _N_SLABS * LANES), const),
            pl.BlockSpec((4, LANES), const),
            pl.BlockSpec((TM, LANES), pos),
            pl.BlockSpec((TM, LANES), pos),
            pl.BlockSpec((TM, LANES), pos),
        ],
        out_specs=out_specs,
        out_shape=out_shapes,
        compiler_params=pltpu.CompilerParams(
            dimension_semantics=("arbitrary",), vmem_limit_bytes=VMEM_LIMIT),
        name="inproj",
    )(x2, anw, w_perm, normw, cos_t, sin_a, sin_b)


def _compress_kernel(x_ref, pos_ref, w1_ref, b1_ref, w2_ref, o_ref):
    x = x_ref[0, 0, 0]
    half = CMP_STRIDE * HEAD_DIM
    xt = (x + pos_ref[0, 0:1, :]).astype(BF16)
    xb = (x + pos_ref[0, 1:2, :]).astype(BF16)
    a = _dot(xt, w1_ref[0, 0:half, :])
    b = _dot(xb, w1_ref[0, half:2 * half, :])
    n = x.shape[0]
    hid = a + pltpu.roll(b, n - 1, 0) + b1_ref[0]
    act = jax.nn.gelu(hid)
    o_ref[0, 0, 0] = _dot(act.astype(BF16), w2_ref[0]).astype(o_ref.dtype)


def _compress(x16, pos2, w1, b1, w2d):
    _, b, g, n, width = x16.shape
    return pl.pallas_call(
        _compress_kernel,
        grid=(2, b, g),
        in_specs=[
            pl.BlockSpec((1, 1, 1, n, width), lambda a, i, j: (a, i, j, 0, 0)),
            pl.BlockSpec((1, 2, width), lambda a, i, j: (a, 0, 0)),
            pl.BlockSpec((1, 2 * width, CMP_HIDDEN), lambda a, i, j: (a, 0, 0)),
            pl.BlockSpec((1, 1, CMP_HIDDEN), lambda a, i, j: (a, 0, 0)),
            pl.BlockSpec((1, CMP_HIDDEN, LANES), lambda a, i, j: (a, 0, 0)),
        ],
        out_specs=pl.BlockSpec((1, 1, 1, n, LANES), lambda a, i, j: (a, i, j, 0, 0)),
        out_shape=jax.ShapeDtypeStruct((2, b, g, n, LANES), BF16),
        compiler_params=pltpu.CompilerParams(
            dimension_semantics=("arbitrary", "arbitrary", "arbitrary"), vmem_limit_bytes=VMEM_LIMIT),
        name="compress",
    )(x16, pos2, w1, b1, w2d)


def _gate_lane(group, pair, parity, branch):
    return (1 - parity) * HEAD_DIM + group * (N_BRANCH * HEADS_PER_GROUP // 2) + branch * (HEADS_PER_GROUP // 2) + pair


def _attn_kernel(q_ref, ks_ref, kw_ref, vs_ref, vw_ref, ck_ref, cv_ref, gate_ref, e_ref, ovt_ref, eb_ref,
                 o_ref, kks, kkw, vs_lo, vs_hi, vw_lo, vw_hi, qst, sel_ref, m_ref, acc_s, acc_w):
    g = pl.program_id(1)
    i = pl.program_id(2)
    seq = ks_ref.shape[0]
    hq = HEADS_PER_GROUP * TQ
    n_pairs = HEADS_PER_GROUP // 2

    @pl.when(i == 0)
    def _prep():
        lane = lax.broadcasted_iota(jnp.int32, (seq, LANES), 1)
        mine = (lane >= HEAD_DIM).astype(jnp.int32) == g
        lo = lane < HEAD_DIM

        def both_halves(src):
            x = src[...]
            xr = jnp.concatenate([x[:, HEAD_DIM:], x[:, :HEAD_DIM]], axis=1)
            return jnp.where(mine, x, xr)

        kks[...] = both_halves(ks_ref)
        kkw[...] = both_halves(kw_ref)
        for src, dst_lo, dst_hi in ((vs_ref, vs_lo, vs_hi), (vw_ref, vw_lo, vw_hi)):
            vv = both_halves(src)
            one = jnp.ones_like(vv)
            dst_lo[...] = jnp.where(lo, vv, one)
            dst_hi[...] = jnp.where(lo, one, vv)

    lane_q = lax.broadcasted_iota(jnp.int32, (TQ, LANES), 1)
    lo_q = lane_q < HEAD_DIM
    for p in range(n_pairs):
        qp = q_ref[:, p * LANES:(p + 1) * LANES]
        zero = jnp.zeros_like(qp)
        qst[p * TQ:(p + 1) * TQ, :] = jnp.where(lo_q, qp, zero)
        qst[(n_pairs + p) * TQ:(n_pairs + p + 1) * TQ, :] = jnp.where(lo_q, zero, qp)

    row = lax.broadcasted_iota(jnp.int32, (TQ, LANES), 0)
    tq = i * TQ + row

    sc = _dot_nt(qst[...], ck_ref[0, 0, 0]).reshape(HEADS_PER_GROUP, TQ, N_CMP_PAD)
    cmp_mask = (lane_q * CMP_STRIDE + (CMP_BLOCK - 1)) <= tq
    s = jnp.where(cmp_mask[None], sc, NEG_INF)
    m = jnp.max(s, axis=-1, keepdims=True)
    e = jnp.where(cmp_mask[None], jnp.exp2(s - m), 0.0)
    l = jnp.sum(e, axis=-1, keepdims=True)
    pc = e * jnp.where(l > 0.0, 1.0 / l, 0.0)
    o_c = _dot(pc.reshape(hq, N_CMP_PAD).astype(BF16), cv_ref[0, 0, 0])

    pcs = jnp.sum(pc, axis=0)
    pcs_hi = pcs.astype(BF16)
    pcs_lo = (pcs - pcs_hi.astype(F32)).astype(BF16)
    imp = _dot_nt(ovt_ref[...], pcs_hi) + _dot_nt(ovt_ref[...], pcs_lo)
    n_blk = seq // SLC_BLOCK
    blk = lax.broadcasted_iota(jnp.int32, (n_blk, TQ), 0)
    tq_t = i * TQ + lax.broadcasted_iota(jnp.int32, (n_blk, TQ), 1)
    cur = lax.shift_right_logical(tq_t, SLC_BLOCK.bit_length() - 1)
    forced = (blk == 0) | (blk == cur) | (blk == cur - 1)
    valid = blk * SLC_BLOCK <= tq_t
    score = jnp.where(valid, imp + jnp.where(forced, FORCE_BONUS, 0.0), NEG_INF)
    blk_f = blk.astype(F32)
    sel_t = jnp.zeros((n_blk, TQ), F32)
    for _ in range(SLC_TOP_N):
        mx = jnp.max(score, axis=0, keepdims=True)
        first = jnp.min(jnp.where(score == mx, blk_f, float(n_blk)), axis=0, keepdims=True)
        pick = blk_f == first
        sel_t = jnp.where(pick, 1.0, sel_t)
        score = jnp.where(pick, -jnp.inf, score)
    sel_pad = jnp.concatenate([sel_t, jnp.zeros((N_SLC_PAD - n_blk, TQ), F32)], axis=0)
    sel_ref[...] = sel_pad.T.astype(BF16)

    col = lax.broadcasted_iota(jnp.int32, (TQ, TQ), 1)
    rowk = lax.broadcasted_iota(jnp.int32, (TQ, TQ), 0)
    causal_bias = jnp.where(col <= rowk, 0.0, NEG_INF)
    tail_bias = jnp.where(col > rowk, 0.0, NEG_INF)

    def tile(ref, kt):
        return ref[pl.ds(pl.multiple_of(kt * TQ, TQ), TQ), :]

    def step(k_ref, v_lo_ref, v_hi_ref, acc_ref, kt, bias, first):
        k_tile = tile(k_ref, kt)
        v_lo = tile(v_lo_ref, kt)
        v_hi = tile(v_hi_ref, kt)
        for hb in range(HEADS_PER_GROUP):
            rows = slice(hb * TQ, (hb + 1) * TQ)
            s = _dot_nt(qst[rows, :], k_tile)
            s0 = s[:, :LANES]
            s1 = s[:, LANES:]
            if bias is not None:
                s0 = s0 + bias[:, :LANES]
                s1 = s1 + bias[:, LANES:]
            m_new = jnp.max(jnp.maximum(s0, s1), axis=-1, keepdims=True)
            if first:
                m_new = jnp.broadcast_to(m_new, (TQ, LANES))
            else:
                m_old = m_ref[rows, :]
                m_new = jnp.maximum(m_old, m_new)
            p = jnp.concatenate([jnp.exp2(s0 - m_new), jnp.exp2(s1 - m_new)], axis=1).astype(BF16)
            pv = _dot(p, v_lo if hb < n_pairs else v_hi)
            if first:
                acc_ref[rows, :] = pv
            else:
                acc_ref[rows, :] = jnp.exp2(m_old - m_new) * acc_ref[rows, :] + pv
            m_ref[rows, :] = m_new

    def sel_bias(kt):
        return (_dot(sel_ref[...], eb_ref[kt]) - 1.0) * (-NEG_INF)

    def gather_denominators(l_gate, acc_ref, br):
        for hb in range(HEADS_PER_GROUP):
            lane_idx = _gate_lane(g, hb % n_pairs, hb // n_pairs, br)
            l_gate = jnp.where(lane_q == lane_idx, acc_ref[hb * TQ:(hb + 1) * TQ, :], l_gate)
        return l_gate

    step(kkw, vw_lo, vw_hi, acc_w, i, causal_bias, True)

    @pl.when(i >= 1)
    def _w1():
        step(kkw, vw_lo, vw_hi, acc_w, i - 1, None, False)

    @pl.when(i >= 2)
    def _w2():
        step(kkw, vw_lo, vw_hi, acc_w, i - 2, tail_bias, False)

    l_gate = gather_denominators(jnp.ones((TQ, LANES), F32), acc_w, 2)

    step(kks, vs_lo, vs_hi, acc_s, i, sel_bias(i) + causal_bias, True)

    def sel_body(kt, carry):
        step(kks, vs_lo, vs_hi, acc_s, kt, sel_bias(kt), False)
        return carry

    lax.fori_loop(0, i, sel_body, 0)
    l_gate = gather_denominators(l_gate, acc_s, 1)

    gate = gate_ref[...] * (1.0 / l_gate)
    g_hi = gate.astype(BF16)
    g_lo = (gate - g_hi.astype(F32)).astype(BF16)
    g_cat = jnp.concatenate([g_hi, g_lo], axis=1)
    out = None
    for br, acc in enumerate((o_c, acc_s, acc_w)):
        g_exp = _dot(g_cat, e_ref[0, br])
        pairs = [jnp.where(lo_q, acc[p * TQ:(p + 1) * TQ, :], acc[(n_pairs + p) * TQ:(n_pairs + p + 1) * TQ, :])
                 for p in range(n_pairs)]
        term = g_exp * jnp.concatenate(pairs, axis=1)
        out = term if out is None else out + term
    o_ref[...] = out.astype(o_ref.dtype)


def _attention(q, ks, kw, vs, vw, cmp_kv, gate, e_mat, ov_mat, eb_mat, batch, seq):
    nq = seq // TQ
    gw = HEADS_PER_GROUP * HEAD_DIM
    hq = HEADS_PER_GROUP * TQ
    kv_spec = pl.BlockSpec((seq, LANES), lambda b, g, i: (b, 0))
    return pl.pallas_call(
        _attn_kernel,
        grid=(batch, N_KV_GROUPS, nq),
        in_specs=[
            pl.BlockSpec((TQ, gw), lambda b, g, i: (b * nq + i, g)),
            kv_spec, kv_spec, kv_spec, kv_spec,
            pl.BlockSpec((1, 1, 1, N_CMP_PAD, LANES), lambda b, g, i: (0, b, g, 0, 0)),
            pl.BlockSpec((1, 1, 1, N_CMP_PAD, LANES), lambda b, g, i: (1, b, g, 0, 0)),
            pl.BlockSpec((TQ, LANES), lambda b, g, i: (b * nq + i, 0)),
            pl.BlockSpec((1, N_BRANCH, 2 * LANES, gw), lambda b, g, i: (g, 0, 0, 0)),
            pl.BlockSpec((seq // SLC_BLOCK, N_CMP_PAD), lambda b, g, i: (0, 0)),
            pl.BlockSpec((nq, N_SLC_PAD, TQ), lambda b, g, i: (0, 0, 0)),
        ],
        out_specs=pl.BlockSpec((TQ, gw), lambda b, g, i: (b * nq + i, g)),
        out_shape=jax.ShapeDtypeStruct((batch * seq, ATTN_WIDTH), BF16),
        scratch_shapes=[
            pltpu.VMEM((seq, LANES), BF16), pltpu.VMEM((seq, LANES), BF16),
            pltpu.VMEM((seq, LANES), BF16), pltpu.VMEM((seq, LANES), BF16),
            pltpu.VMEM((seq, LANES), BF16), pltpu.VMEM((seq, LANES), BF16),
            pltpu.VMEM((hq, LANES), BF16),
            pltpu.VMEM((TQ, N_SLC_PAD), BF16),
            pltpu.VMEM((hq, LANES), F32),
            pltpu.VMEM((hq, LANES), F32), pltpu.VMEM((hq, LANES), F32),
        ],
        compiler_params=pltpu.CompilerParams(
            dimension_semantics=("arbitrary", "arbitrary", "arbitrary"), vmem_limit_bytes=VMEM_LIMIT),
        name="nsa_attention",
    )(q, ks, kw, vs, vw, cmp_kv, cmp_kv, gate, e_mat, ov_mat, eb_mat)


def _merge_kernel(x_ref, up_ref, at_ref, mg_ref, pw_ref, ps_ref, wpb_ref, wab_ref, wo_ref,
                  o_ref, halo_ref, *, tiles_per_seq):
    i = pl.program_id(0)
    tm = x_ref.shape[0]
    u = up_ref[...]

    @pl.when((i % tiles_per_seq) == 0)
    def _seq_start():
        halo_ref[...] = jnp.zeros_like(halo_ref)

    halo = halo_ref[...]
    halo_ref[...] = u[tm - POOL_HALO:, :]
    ue = jnp.concatenate([halo, u], axis=0)
    rowp = lax.broadcasted_iota(jnp.int32, (tm, POOL_GROUP), 0) + (i % tiles_per_seq) * tm
    ys = []
    for gi, w in enumerate(POOL_WINDOWS):
        xg = ue[:, gi * POOL_GROUP:(gi + 1) * POOL_GROUP]
        sw = xg
        k = 1
        while k < w:
            sw = sw + pltpu.roll(sw, k, 0)
            k *= 2
        cnt = jnp.minimum(rowp + 1, w).astype(F32)
        ug = u[:, gi * POOL_GROUP:(gi + 1) * POOL_GROUP]
        pooled = sw[POOL_HALO:, :] / cnt - ug
        ys.append(_dot(pooled.astype(BF16), pw_ref[gi]))
    y = jnp.concatenate(ys, axis=1) * ps_ref[...]
    y_pool = _dot(y.astype(BF16), wpb_ref[...])
    y_attn = _dot(at_ref[...], wab_ref[...])
    mg = mg_ref[...]
    merged = mg[:, :D_MODEL] * y_pool + mg[:, D_MODEL:] * y_attn
    o_ref[...] = x_ref[...] + _dot(merged.astype(BF16), wo_ref[...])


def _merge(x2, u_pool, attn, mg, pool_w, pool_scale, w_pool_br, w_attn_br, w_o, seq):
    t = x2.shape[0]
    row = lambda i: (i, 0)
    const = lambda i: (0, 0)
    return pl.pallas_call(
        functools.partial(_merge_kernel, tiles_per_seq=seq // TM),
        grid=(t // TM,),
        in_specs=[
            pl.BlockSpec((TM, D_MODEL), row),
            pl.BlockSpec((TM, POOL_WIDTH), row),
            pl.BlockSpec((TM, ATTN_WIDTH), row),
            pl.BlockSpec((TM, 2 * D_MODEL), row),
            pl.BlockSpec((N_POOL_GROUPS, POOL_GROUP, POOL_GROUP), lambda i: (0, 0, 0)),
            pl.BlockSpec((1, POOL_WIDTH), const),
            pl.BlockSpec((POOL_WIDTH, D_MODEL), const),
            pl.BlockSpec((ATTN_WIDTH, D_MODEL), const),
            pl.BlockSpec((D_MODEL, D_MODEL), const),
        ],
        out_specs=pl.BlockSpec((TM, D_MODEL), row),
        out_shape=jax.ShapeDtypeStruct((t, D_MODEL), F32),
        scratch_shapes=[pltpu.VMEM((POOL_HALO, POOL_WIDTH), F32)],
        compiler_params=pltpu.CompilerParams(
            dimension_semantics=("arbitrary",), vmem_limit_bytes=VMEM_LIMIT),
        name="pool_merge",
    )(x2, u_pool, attn, mg, pool_w, pool_scale, w_pool_br, w_attn_br, w_o)


def _ffn_kernel(x_ref, nw_ref, wup_ref, cw_ref, cb_ref, wdn_ref, o_ref, halo_ref, acc_ref, *, tiles_per_seq):
    i = pl.program_id(0)
    tm = x_ref.shape[0]
    x = x_ref[...]
    ms = jnp.mean(x * x, axis=-1, keepdims=True)
    h = ((x * lax.rsqrt(ms + EPS)) * nw_ref[...]).astype(BF16)

    @pl.when((i % tiles_per_seq) == 0)
    def _seq_start():
        halo_ref[...] = jnp.zeros_like(halo_ref)

    def conv(c):
        u = _dot(h, wup_ref[c])
        halo = halo_ref[c]
        halo_ref[c] = u[tm - CONV_HALO:, :]
        ue = jnp.concatenate([halo, u], axis=0)
        w = cw_ref[c]
        u1 = pltpu.roll(ue, 1, 0)[CONV_HALO:, :]
        u2 = pltpu.roll(ue, 2, 0)[CONV_HALO:, :]
        return cb_ref[c] + w[0:1, :] * u2 + w[1:2, :] * u1 + w[2:3, :] * u

    for c in range(N_FF_CHUNKS):
        gate = conv(c)
        val = conv(N_FF_CHUNKS + c)
        act = (gate * jax.nn.sigmoid(gate)) * val
        part = _dot(act.astype(BF16), wdn_ref[c])
        if c == 0:
            acc_ref[...] = part
        else:
            acc_ref[...] += part
    o_ref[...] = x + acc_ref[...]


def _ffn(x1, nw, wup, cw, cb, wdn, seq):
    t = x1.shape[0]
    row = lambda i: (i, 0)
    c3 = lambda i: (0, 0, 0)
    return pl.pallas_call(
        functools.partial(_ffn_kernel, tiles_per_seq=seq // TM),
        grid=(t // TM,),
        in_specs=[
            pl.BlockSpec((TM, D_MODEL), row),
            pl.BlockSpec((1, D_MODEL), lambda i: (0, 0)),
            pl.BlockSpec((2 * N_FF_CHUNKS, D_MODEL, FF_CHUNK), c3),
            pl.BlockSpec((2 * N_FF_CHUNKS, CONV_HALO, FF_CHUNK), c3),
            pl.BlockSpec((2 * N_FF_CHUNKS, 1, FF_CHUNK), c3),
            pl.BlockSpec((N_FF_CHUNKS, FF_CHUNK, D_MODEL), c3),
        ],
        out_specs=pl.BlockSpec((TM, D_MODEL), row),
        out_shape=jax.ShapeDtypeStruct((t, D_MODEL), F32),
        scratch_shapes=[pltpu.VMEM((2 * N_FF_CHUNKS, CONV_HALO, FF_CHUNK), F32),
                        pltpu.VMEM((TM, D_MODEL), F32)],
        compiler_params=pltpu.CompilerParams(
            dimension_semantics=("arbitrary",), vmem_limit_bytes=VMEM_LIMIT),
        name="conv_ffn",
    )(x1, nw, wup, cw, cb, wdn)


def _w_in_columns():
    cols = list(range(0, OFF_KV))
    for kv in range(2):
        for br in range(N_BRANCH):
            base = OFF_KV + br * 4 * HEAD_DIM + kv * 2 * HEAD_DIM
            cols += list(range(base, base + 2 * HEAD_DIM))
    cols += list(range(OFF_MERGE_G, OFF_MERGE_G + 2 * D_MODEL))
    gate_cols = np.full((LANES,), -1, np.int64)
    for h in range(N_HEADS):
        for br in range(N_BRANCH):
            lane = _gate_lane(h // HEADS_PER_GROUP, (h % HEADS_PER_GROUP) // 2, h % 2, br)
            gate_cols[lane] = OFF_NSA_G + h * N_BRANCH + br
    return np.asarray(cols, np.int32), gate_cols


def _rope_tables(seq):
    half = HEAD_DIM // 2
    freqs = ROPE_THETA ** (-jnp.arange(half, dtype=F32) / half)
    ang = jnp.arange(seq, dtype=F32)[:, None] * freqs[None, :]
    cos = jnp.cos(ang)
    sin = jnp.sin(ang)
    zero = jnp.zeros_like(sin)
    cos_t = jnp.tile(cos, (1, LANES // half))
    sin_a = jnp.tile(jnp.concatenate([-sin, zero], axis=1), (1, LANES // HEAD_DIM))
    sin_b = jnp.tile(jnp.concatenate([zero, sin], axis=1), (1, LANES // HEAD_DIM))
    return cos_t, sin_a, sin_b


def _selection_constants(seq):
    n_cmp = (seq - CMP_BLOCK) // CMP_STRIDE + 1
    n_slc = seq // SLC_BLOCK
    ci = np.arange(N_CMP_PAD)[:, None]
    sj = np.arange(N_SLC_PAD)[None, :]
    ov = ((ci * CMP_STRIDE < (sj + 1) * SLC_BLOCK) & (ci * CMP_STRIDE + CMP_BLOCK > sj * SLC_BLOCK)
          & (ci < n_cmp) & (sj < n_slc))
    ovt = ov.T[:n_slc]
    eb = (np.arange(seq)[None, :] // SLC_BLOCK) == np.arange(N_SLC_PAD)[:, None]
    eb = eb.reshape(N_SLC_PAD, seq // TQ, TQ).transpose(1, 0, 2)
    gw = HEADS_PER_GROUP * HEAD_DIM
    e = np.zeros((N_KV_GROUPS, N_BRANCH, 2 * LANES, gw), np.float32)
    for g in range(N_KV_GROUPS):
        for br in range(N_BRANCH):
            for h in range(HEADS_PER_GROUP):
                lane = _gate_lane(g, h // 2, h % 2, br)
                e[g, br, lane, h * HEAD_DIM:(h + 1) * HEAD_DIM] = 1.0
                e[g, br, LANES + lane, h * HEAD_DIM:(h + 1) * HEAD_DIM] = 1.0
    return (jnp.asarray(ovt, BF16), jnp.asarray(eb, BF16), jnp.asarray(e, BF16))


def _chunk_cols(w, n_chunks):
    return jnp.moveaxis(w.reshape(w.shape[:-1] + (n_chunks, FF_CHUNK)), -2, 0)


def _layer(x, attn_norm_w, w_in, pool_w, pool_scale, q_norm_w, k_norm_w, cmp_pos, cmp_w1, cmp_b1, cmp_w2,
           w_pool_br, w_attn_br, w_o, ffn_norm_w, w_up, conv_w, conv_b, w_down):
    batch, seq, _ = x.shape
    t = batch * seq
    x2 = x.reshape(t, D_MODEL)

    cols, gate_cols = _w_in_columns()
    w_gate = jnp.where(gate_cols[None, :] >= 0, jnp.take(w_in, np.maximum(gate_cols, 0), axis=1), 0.0)
    w_perm = jnp.concatenate([jnp.take(w_in, cols, axis=1), w_gate], axis=1).astype(BF16)
    normw = jnp.concatenate([jnp.tile(q_norm_w[None, :], (1, 2)), jnp.tile(k_norm_w, (1, 2))], axis=0)
    cos_t, sin_a, sin_b = _rope_tables(seq)
    u_pool, q, kc, ks, kw, vc, vs, vw, mg, gate = _inproj(
        x2, attn_norm_w[None, :], w_perm, normw, cos_t, sin_a, sin_b, seq)

    def rows16(a):
        a = a.reshape(batch, seq // CMP_STRIDE, CMP_STRIDE, N_KV_GROUPS, HEAD_DIM)
        return a.transpose(0, 3, 1, 2, 4).reshape(batch, N_KV_GROUPS, seq // CMP_STRIDE, CMP_STRIDE * HEAD_DIM)

    x16 = jnp.stack([rows16(kc), rows16(vc)], axis=0)
    pos2 = cmp_pos.reshape(2, 2, CMP_STRIDE * HEAD_DIM)
    w2d = jnp.concatenate([cmp_w2, cmp_w2], axis=-1).astype(BF16)
    cmp_kv = _compress(x16, pos2, cmp_w1.astype(BF16), cmp_b1[:, None, :], w2d)

    ov_mat, eb_mat, e_mat = _selection_constants(seq)
    attn = _attention(q, ks, kw, vs, vw, cmp_kv, gate, e_mat, ov_mat, eb_mat, batch, seq)

    x1 = _merge(x2, u_pool, attn, mg, pool_w.astype(BF16), pool_scale[None, :], w_pool_br.astype(BF16),
                w_attn_br.astype(BF16), w_o.astype(BF16), seq)

    pad = D_FF_PAD - D_FF

    def pad_halves(w):
        lead = [(0, 0)] * (w.ndim - 1)
        return jnp.concatenate([jnp.pad(w[..., :D_FF], lead + [(0, pad)]),
                                jnp.pad(w[..., D_FF:], lead + [(0, pad)])], axis=-1)

    wup = _chunk_cols(pad_halves(w_up).astype(BF16), 2 * N_FF_CHUNKS)
    cw = _chunk_cols(jnp.pad(pad_halves(conv_w), ((0, CONV_HALO - CONV_WIDTH), (0, 0))), 2 * N_FF_CHUNKS)
    cb = _chunk_cols(pad_halves(conv_b)[None, :], 2 * N_FF_CHUNKS)
    wdn = jnp.pad(w_down, ((0, pad), (0, 0))).astype(BF16).reshape(N_FF_CHUNKS, FF_CHUNK, D_MODEL)
    out = _ffn(x1, ffn_norm_w[None, :], wup, cw, cb, wdn, seq)
    return out.reshape(batch, seq, D_MODEL)


def kernel(x, attn_norm_w, w_in, pool_w, pool_scale, q_norm_w, k_norm_w, cmp_pos, cmp_w1, cmp_b1, cmp_w2,
           w_pool_br, w_attn_br, w_o, ffn_norm_w, w_up, conv_w, conv_b, w_down):
    for l in range(attn_norm_w.shape[0]):
        x = _layer(x, attn_norm_w[l], w_in[l], pool_w[l], pool_scale[l], q_norm_w[l], k_norm_w[l],
                   cmp_pos[l], cmp_w1[l], cmp_b1[l], cmp_w2[l], w_pool_br[l], w_attn_br[l], w_o[l],
                   ffn_norm_w[l], w_up[l], conv_w[l], conv_b[l], w_down[l])
    return x
```

```python
import functools

import numpy as np
import jax
import jax.numpy as jnp
from jax import lax
from jax.experimental import pallas as pl
from jax.experimental.pallas import tpu as pltpu

D_MODEL = 1024
POOL_WINDOWS = (2, 4, 8, 16)
N_POOL_GROUPS = 4
POOL_GROUP = 128
POOL_WIDTH = 512
HEAD_DIM = 64
N_HEADS = 16
N_KV_GROUPS = 2
HEADS_PER_GROUP = 8
ATTN_WIDTH = 1024
N_BRANCH = 3
KV_WIDTH = 768
CMP_BLOCK = 32
CMP_STRIDE = 16
CMP_HIDDEN = 128
SLC_BLOCK = 64
SLC_TOP_N = 8
WINDOW = 512
ROPE_THETA = 10000.0
SCALE = HEAD_DIM ** -0.5
FORCE_BONUS = 1000.0
NEG_INF = -1e30
OFF_Q = POOL_WIDTH
OFF_KV = OFF_Q + ATTN_WIDTH
OFF_NSA_G = OFF_KV + KV_WIDTH
OFF_MERGE_G = OFF_NSA_G + N_BRANCH * N_HEADS
D_FF = 2752
CONV_WIDTH = 3
EPS = 1e-6
LOG2E = 1.4426950408889634

LANES = 128
MXU_N = 256
VMEM_LIMIT = 56 * 1024 * 1024

TM = 512
TQ = 256
D_FF_PAD = 2816
FF_CHUNK = 256
N_FF_CHUNKS = D_FF_PAD // FF_CHUNK
POOL_HALO = 16
CONV_HALO = 8
N_CMP_PAD = 128
N_SLC_PAD = 128

F32 = jnp.float32
BF16 = jnp.bfloat16


def _dot(a, b):
    return jnp.dot(a, b, preferred_element_type=F32)


def _dot_nt(a, b):
    return lax.dot_general(a, b, (((1,), (1,)), ((), ())), preferred_element_type=F32)


def _split_dot(a, b):
    hi = a.astype(BF16)
    lo = (a - hi.astype(F32)).astype(BF16)
    return _dot(hi, b) + _dot(lo, b)


_SLAB_POOL = 0
_SLAB_Q = 4
_SLAB_K = 12
_SLAB_V = 15
_SLAB_MERGE = 18
_SLAB_GATE = 34
_N_SLABS = 35


def _inproj_kernel(x_ref, anw_ref, w_ref, nw_ref, cos_ref, sa_ref, sb_ref,
                   up_ref, q_ref, kvc_ref, ks_ref, kw_ref, vs_ref, vw_ref, mg_ref, g_ref):
    x = x_ref[...]
    ms = jnp.mean(x * x, axis=-1, keepdims=True)
    h = ((x * lax.rsqrt(ms + EPS)) * anw_ref[...]).astype(BF16)
    tm = x.shape[0]
    lane = lax.broadcasted_iota(jnp.int32, (tm, LANES), 1)
    lo = lane < HEAD_DIM
    cos = cos_ref[...]
    sa = sa_ref[...]
    sb = sb_ref[...]

    def norm_rope(z, nw, scale):
        ss = z * z
        s_lo = jnp.sum(jnp.where(lo, ss, 0.0), axis=-1, keepdims=True)
        s_hi = jnp.sum(jnp.where(lo, 0.0, ss), axis=-1, keepdims=True)
        msq = jnp.where(lo, s_lo, s_hi) * (1.0 / HEAD_DIM)
        y = (z * lax.rsqrt(msq + EPS)) * nw
        r = y * cos + pltpu.roll(y, LANES - HEAD_DIM // 2, 1) * sa + pltpu.roll(y, HEAD_DIM // 2, 1) * sb
        return r * scale if scale is not None else r

    k_refs = (None, ks_ref, kw_ref)
    v_refs = (None, vs_ref, vw_ref)

    def emit(slab, z):
        if slab < _SLAB_Q:
            c = (slab - _SLAB_POOL) * LANES
            up_ref[:, c:c + LANES] = z
        elif slab < _SLAB_K:
            c = (slab - _SLAB_Q) * LANES
            q_ref[:, c:c + LANES] = norm_rope(z, nw_ref[0:1, :], SCALE * LOG2E).astype(BF16)
        elif slab < _SLAB_V:
            br = slab - _SLAB_K
            r = norm_rope(z, nw_ref[1 + br:2 + br, :], None)
            if br == 0:
                kvc_ref[:, 0:LANES] = r
            else:
                k_refs[br][...] = r.astype(BF16)
        elif slab < _SLAB_MERGE:
            br = slab - _SLAB_V
            if br == 0:
                kvc_ref[:, LANES:2 * LANES] = z
            else:
                v_refs[br][...] = z.astype(BF16)
        elif slab < _SLAB_GATE:
            c = (slab - _SLAB_MERGE) * LANES
            mg_ref[:, c:c + LANES] = jax.nn.sigmoid(z).astype(BF16)
        else:
            g_ref[...] = jax.nn.sigmoid(z)

    slab = 0
    while slab < _N_SLABS:
        width = 2 if slab + 1 < _N_SLABS else 1
        z = _dot(h, w_ref[:, slab * LANES:(slab + width) * LANES])
        for j in range(width):
            emit(slab + j, z[:, j * LANES:(j + 1) * LANES])
        slab += width


def _inproj(x2, anw, w_perm, normw, cos_t, sin_a, sin_b, seq):
    t = x2.shape[0]
    tiles_per_seq = seq // TM
    row = lambda i: (i, 0)
    const = lambda i: (0, 0)
    pos = lambda i: (i % tiles_per_seq, 0)
    out_shapes = (
        jax.ShapeDtypeStruct((t, POOL_WIDTH), F32),
        jax.ShapeDtypeStruct((t, ATTN_WIDTH), BF16),
        jax.ShapeDtypeStruct((t, 2 * LANES), F32),
        jax.ShapeDtypeStruct((t, LANES), BF16),
        jax.ShapeDtypeStruct((t, LANES), BF16),
        jax.ShapeDtypeStruct((t, LANES), BF16),
        jax.ShapeDtypeStruct((t, LANES), BF16),
        jax.ShapeDtypeStruct((t, 2 * D_MODEL), BF16),
        jax.ShapeDtypeStruct((t, LANES), F32),
    )
    out_specs = tuple(pl.BlockSpec((TM, s.shape[1]), row) for s in out_shapes)
    return pl.pallas_call(
        _inproj_kernel,
        grid=(t // TM,),
        in_specs=[
            pl.BlockSpec((TM, D_MODEL), row),
            pl.BlockSpec((1, D_MODEL), const),
            pl.BlockSpec((D_MODEL, _N_SLABS * LANES), const),
            pl.BlockSpec((4, LANES), const),
            pl.BlockSpec((TM, LANES), pos),
            pl.BlockSpec((TM, LANES), pos),
            pl.BlockSpec((TM, LANES), pos),
        ],
        out_specs=out_specs,
        out_shape=out_shapes,
        compiler_params=pltpu.CompilerParams(
            dimension_semantics=("arbitrary",), vmem_limit_bytes=VMEM_LIMIT),
        name="inproj",
    )(x2, anw, w_perm, normw, cos_t, sin_a, sin_b)


def _compress_kernel(x_ref, pos_ref, w1_ref, b1_ref, w2_ref, o_ref):
    n = x_ref.shape[0] // CMP_STRIDE
    top = None
    bot = None
    for l in range(CMP_STRIDE):
        y = x_ref[pl.ds(l, n, stride=CMP_STRIDE), :]
        t = _dot((y + pos_ref[0, l:l + 1, :]).astype(BF16), w1_ref[0, l])
        b = _dot((y + pos_ref[0, CMP_STRIDE + l:CMP_STRIDE + l + 1, :]).astype(BF16), w1_ref[0, CMP_STRIDE + l])
        top = t if top is None else top + t
        bot = b if bot is None else bot + b
    hid = top + pltpu.roll(bot, n - 1, 0) + b1_ref[0]
    act = jax.nn.gelu(hid).astype(BF16)
    for g in range(N_KV_GROUPS):
        o_ref[0, 0, g] = _dot(act[:, g * CMP_HIDDEN:(g + 1) * CMP_HIDDEN], w2_ref[0]).astype(o_ref.dtype)


def _compress(kvc, pos128, w1bd, b1, w2d, batch, seq):
    n = seq // CMP_STRIDE
    return pl.pallas_call(
        _compress_kernel,
        grid=(2, batch),
        in_specs=[
            pl.BlockSpec((seq, LANES), lambda a, i: (i, a)),
            pl.BlockSpec((1, CMP_BLOCK, LANES), lambda a, i: (a, 0, 0)),
            pl.BlockSpec((1, CMP_BLOCK, LANES, N_KV_GROUPS * CMP_HIDDEN), lambda a, i: (a, 0, 0, 0)),
            pl.BlockSpec((1, 1, N_KV_GROUPS * CMP_HIDDEN), lambda a, i: (a, 0, 0)),
            pl.BlockSpec((1, CMP_HIDDEN, LANES), lambda a, i: (a, 0, 0)),
        ],
        out_specs=pl.BlockSpec((1, 1, N_KV_GROUPS, n, LANES), lambda a, i: (a, i, 0, 0, 0)),
        out_shape=jax.ShapeDtypeStruct((2, batch, N_KV_GROUPS, n, LANES), BF16),
        compiler_params=pltpu.CompilerParams(
            dimension_semantics=("arbitrary", "arbitrary"), vmem_limit_bytes=VMEM_LIMIT),
        name="compress",
    )(kvc, pos128, w1bd, b1, w2d)


def _gate_lane(group, pair, parity, branch):
    return (1 - parity) * HEAD_DIM + group * (N_BRANCH * HEADS_PER_GROUP // 2) + branch * (HEADS_PER_GROUP // 2) + pair


def _attn_kernel(q_ref, ks_ref, kw_ref, vs_ref, vw_ref, ck_ref, cv_ref, gate_ref, e_ref, ovt_ref, eb_ref,
                 o_ref, kks, kkw, vs1, vw1, qst, sel_ref, m_s, m_w, acc_s, acc_w):
    g = pl.program_id(1)
    i = pl.program_id(2)
    seq = ks_ref.shape[0]
    hq = HEADS_PER_GROUP * TQ
    n_pairs = HEADS_PER_GROUP // 2

    @pl.when(i == 0)
    def _prep():
        lane = lax.broadcasted_iota(jnp.int32, (seq, LANES), 1)
        mine = (lane >= HEAD_DIM).astype(jnp.int32) == g
        lo = lane < HEAD_DIM

        def both_halves(src):
            x = src[...]
            xr = jnp.concatenate([x[:, HEAD_DIM:], x[:, :HEAD_DIM]], axis=1)
            return jnp.where(mine, x, xr)

        kks[...] = both_halves(ks_ref)
        kkw[...] = both_halves(kw_ref)
        for src, dst in ((vs_ref, vs1), (vw_ref, vw1)):
            vv = both_halves(src)
            one = jnp.ones_like(vv)
            dst[:, :LANES] = jnp.where(lo, vv, one)
            dst[:, LANES:] = jnp.where(lo, one, vv)

    lane_q = lax.broadcasted_iota(jnp.int32, (TQ, LANES), 1)
    lo_q = lane_q < HEAD_DIM
    for p in range(n_pairs):
        qp = q_ref[:, p * LANES:(p + 1) * LANES]
        zero = jnp.zeros_like(qp)
        qst[p * TQ:(p + 1) * TQ, :] = jnp.where(lo_q, qp, zero)
        qst[(n_pairs + p) * TQ:(n_pairs + p + 1) * TQ, :] = jnp.where(lo_q, zero, qp)

    row = lax.broadcasted_iota(jnp.int32, (TQ, LANES), 0)
    tq = i * TQ + row

    sc = _dot_nt(qst[...], ck_ref[0, 0, 0]).reshape(HEADS_PER_GROUP, TQ, N_CMP_PAD)
    cmp_mask = (lane_q * CMP_STRIDE + (CMP_BLOCK - 1)) <= tq
    s = jnp.where(cmp_mask[None], sc, NEG_INF)
    m = jnp.max(s, axis=-1, keepdims=True)
    e = jnp.where(cmp_mask[None], jnp.exp2(s - m), 0.0)
    l = jnp.sum(e, axis=-1, keepdims=True)
    pc = e * jnp.where(l > 0.0, 1.0 / l, 0.0)
    o_c = _dot(pc.reshape(hq, N_CMP_PAD).astype(BF16), cv_ref[0, 0, 0])

    pcs = jnp.sum(pc, axis=0)
    pcs_hi = pcs.astype(BF16)
    pcs_lo = (pcs - pcs_hi.astype(F32)).astype(BF16)
    imp = _dot_nt(ovt_ref[...], pcs_hi) + _dot_nt(ovt_ref[...], pcs_lo)
    n_blk = seq // SLC_BLOCK
    blk = lax.broadcasted_iota(jnp.int32, (n_blk, TQ), 0)
    tq_t = i * TQ + lax.broadcasted_iota(jnp.int32, (n_blk, TQ), 1)
    cur = lax.shift_right_logical(tq_t, SLC_BLOCK.bit_length() - 1)
    forced = (blk == 0) | (blk == cur) | (blk == cur - 1)
    valid = blk * SLC_BLOCK <= tq_t
    score = jnp.where(valid, imp + jnp.where(forced, FORCE_BONUS, 0.0), NEG_INF)
    blk_f = blk.astype(F32)
    sel_t = jnp.zeros((n_blk, TQ), F32)
    for _ in range(SLC_TOP_N):
        mx = jnp.max(score, axis=0, keepdims=True)
        first = jnp.min(jnp.where(score == mx, blk_f, float(n_blk)), axis=0, keepdims=True)
        pick = blk_f == first
        sel_t = jnp.where(pick, 1.0, sel_t)
        score = jnp.where(pick, -jnp.inf, score)
    sel_pad = jnp.concatenate([sel_t, jnp.zeros((N_SLC_PAD - n_blk, TQ), F32)], axis=0)
    sel_ref[...] = sel_pad.T.astype(BF16)

    col = lax.broadcasted_iota(jnp.int32, (TQ, TQ), 1)
    rowk = lax.broadcasted_iota(jnp.int32, (TQ, TQ), 0)
    causal_bias = jnp.where(col <= rowk, 0.0, NEG_INF)
    tail_bias = jnp.where(col > rowk, 0.0, NEG_INF)

    def tile(ref, kt):
        return ref[pl.ds(pl.multiple_of(kt * TQ, TQ), TQ), :]

    def step(k_ref, v_ref, acc_ref, m_ref, kt, bias, first):
        k_tile = tile(k_ref, kt)
        v_tile = tile(v_ref, kt)
        for hb in range(HEADS_PER_GROUP):
            rows = slice(hb * TQ, (hb + 1) * TQ)
            s = _dot_nt(qst[rows, :], k_tile)
            s0 = s[:, :LANES]
            s1 = s[:, LANES:]
            if bias is not None:
                s0 = s0 + bias[:, :LANES]
                s1 = s1 + bias[:, LANES:]
            m_new = jnp.max(jnp.maximum(s0, s1), axis=-1, keepdims=True)
            if first:
                m_new = jnp.broadcast_to(m_new, (TQ, LANES))
            else:
                m_old = m_ref[rows, :]
                m_new = jnp.maximum(m_old, m_new)
            p = jnp.concatenate([jnp.exp2(s0 - m_new), jnp.exp2(s1 - m_new)], axis=1).astype(BF16)
            pv = _dot(p, v_tile)
            pv = pv[:, :LANES] if hb < n_pairs else pv[:, LANES:]
            if first:
                acc_ref[rows, :] = pv
            else:
                acc_ref[rows, :] = jnp.exp2(m_old - m_new) * acc_ref[rows, :] + pv
            m_ref[rows, :] = m_new

    def sel_bias(kt):
        return (_dot(sel_ref[...], eb_ref[kt]) - 1.0) * (-NEG_INF)

    def gather_denominators(l_gate, acc_ref, br):
        for hb in range(HEADS_PER_GROUP):
            lane_idx = _gate_lane(g, hb % n_pairs, hb // n_pairs, br)
            l_gate = jnp.where(lane_q == lane_idx, acc_ref[hb * TQ:(hb + 1) * TQ, :], l_gate)
        return l_gate

    def win_step(kt, bias, first=False):
        step(kkw, vw1, acc_w, m_w, kt, bias, first)

    def sel_step(kt, first=False):
        bias = sel_bias(kt) + causal_bias if first else sel_bias(kt)
        step(kks, vs1, acc_s, m_s, kt, bias, first)

    win_step(i, causal_bias, True)
    sel_step(i, True)

    @pl.when(i == 1)
    def _w1():
        win_step(i - 1, None)

    @pl.when(i >= 2)
    def _w2():
        win_step(i - 1, None)
        win_step(i - 2, tail_bias)

    def sel_pair(j, carry):
        sel_step(2 * j)
        sel_step(2 * j + 1)
        return carry

    lax.fori_loop(0, lax.shift_right_logical(i, 1), sel_pair, 0)

    @pl.when((i & 1) == 1)
    def _sel_odd():
        sel_step(i - 1)

    l_gate = gather_denominators(jnp.ones((TQ, LANES), F32), acc_w, 2)
    l_gate = gather_denominators(l_gate, acc_s, 1)

    gate = gate_ref[...] * (1.0 / l_gate)
    g_hi = gate.astype(BF16)
    g_lo = (gate - g_hi.astype(F32)).astype(BF16)
    g_cat = jnp.concatenate([g_hi, g_lo], axis=1)
    out = None
    for br, acc in enumerate((o_c, acc_s, acc_w)):
        g_exp = _dot(g_cat, e_ref[0, br])
        pairs = [jnp.where(lo_q, acc[p * TQ:(p + 1) * TQ, :], acc[(n_pairs + p) * TQ:(n_pairs + p + 1) * TQ, :])
                 for p in range(n_pairs)]
        term = g_exp * jnp.concatenate(pairs, axis=1)
        out = term if out is None else out + term
    o_ref[...] = out.astype(o_ref.dtype)


def _attention(q, ks, kw, vs, vw, cmp_kv, gate, e_mat, ov_mat, eb_mat, batch, seq):
    nq = seq // TQ
    gw = HEADS_PER_GROUP * HEAD_DIM
    hq = HEADS_PER_GROUP * TQ
    kv_spec = pl.BlockSpec((seq, LANES), lambda b, g, i: (b, 0))
    return pl.pallas_call(
        _attn_kernel,
        grid=(batch, N_KV_GROUPS, nq),
        in_specs=[
            pl.BlockSpec((TQ, gw), lambda b, g, i: (b * nq + i, g)),
            kv_spec, kv_spec, kv_spec, kv_spec,
            pl.BlockSpec((1, 1, 1, N_CMP_PAD, LANES), lambda b, g, i: (0, b, g, 0, 0)),
            pl.BlockSpec((1, 1, 1, N_CMP_PAD, LANES), lambda b, g, i: (1, b, g, 0, 0)),
            pl.BlockSpec((TQ, LANES), lambda b, g, i: (b * nq + i, 0)),
            pl.BlockSpec((1, N_BRANCH, 2 * LANES, gw), lambda b, g, i: (g, 0, 0, 0)),
            pl.BlockSpec((seq // SLC_BLOCK, N_CMP_PAD), lambda b, g, i: (0, 0)),
            pl.BlockSpec((nq, N_SLC_PAD, TQ), lambda b, g, i: (0, 0, 0)),
        ],
        out_specs=pl.BlockSpec((TQ, gw), lambda b, g, i: (b * nq + i, g)),
        out_shape=jax.ShapeDtypeStruct((batch * seq, ATTN_WIDTH), BF16),
        scratch_shapes=[
            pltpu.VMEM((seq, LANES), BF16), pltpu.VMEM((seq, LANES), BF16),
            pltpu.VMEM((seq, 2 * LANES), BF16), pltpu.VMEM((seq, 2 * LANES), BF16),
            pltpu.VMEM((hq, LANES), BF16),
            pltpu.VMEM((TQ, N_SLC_PAD), BF16),
            pltpu.VMEM((hq, LANES), F32), pltpu.VMEM((hq, LANES), F32),
            pltpu.VMEM((hq, LANES), F32), pltpu.VMEM((hq, LANES), F32),
        ],
        compiler_params=pltpu.CompilerParams(
            dimension_semantics=("arbitrary", "arbitrary", "arbitrary"), vmem_limit_bytes=VMEM_LIMIT),
        name="nsa_attention",
    )(q, ks, kw, vs, vw, cmp_kv, cmp_kv, gate, e_mat, ov_mat, eb_mat)


def _merge_kernel(x_ref, up_ref, at_ref, mg_ref, pw_ref, ps_ref, wpb_ref, wab_ref, wo_ref,
                  o_ref, halo_ref, *, tiles_per_seq):
    i = pl.program_id(0)
    tm = x_ref.shape[0]
    u = up_ref[...]

    @pl.when((i % tiles_per_seq) == 0)
    def _seq_start():
        halo_ref[...] = jnp.zeros_like(halo_ref)

    halo = halo_ref[...]
    halo_ref[...] = u[tm - POOL_HALO:, :]
    ue = jnp.concatenate([halo, u], axis=0)
    rowp = lax.broadcasted_iota(jnp.int32, (tm, POOL_GROUP), 0) + (i % tiles_per_seq) * tm
    ys = []
    for gi, w in enumerate(POOL_WINDOWS):
        xg = ue[:, gi * POOL_GROUP:(gi + 1) * POOL_GROUP]
        sw = xg
        k = 1
        while k < w:
            sw = sw + pltpu.roll(sw, k, 0)
            k *= 2
        cnt = jnp.minimum(rowp + 1, w).astype(F32)
        ug = u[:, gi * POOL_GROUP:(gi + 1) * POOL_GROUP]
        pooled = sw[POOL_HALO:, :] / cnt - ug
        ys.append(_dot(pooled.astype(BF16), pw_ref[gi]))
    y = jnp.concatenate(ys, axis=1) * ps_ref[...]
    y_pool = _dot(y.astype(BF16), wpb_ref[...])
    y_attn = _dot(at_ref[...], wab_ref[...])
    mg = mg_ref[...]
    merged = mg[:, :D_MODEL] * y_pool + mg[:, D_MODEL:] * y_attn
    o_ref[...] = x_ref[...] + _dot(merged.astype(BF16), wo_ref[...])


def _merge(x2, u_pool, attn, mg, pool_w, pool_scale, w_pool_br, w_attn_br, w_o, seq):
    t = x2.shape[0]
    row = lambda i: (i, 0)
    const = lambda i: (0, 0)
    return pl.pallas_call(
        functools.partial(_merge_kernel, tiles_per_seq=seq // TM),
        grid=(t // TM,),
        in_specs=[
            pl.BlockSpec((TM, D_MODEL), row),
            pl.BlockSpec((TM, POOL_WIDTH), row),
            pl.BlockSpec((TM, ATTN_WIDTH), row),
            pl.BlockSpec((TM, 2 * D_MODEL), row),
            pl.BlockSpec((N_POOL_GROUPS, POOL_GROUP, POOL_GROUP), lambda i: (0, 0, 0)),
            pl.BlockSpec((1, POOL_WIDTH), const),
            pl.BlockSpec((POOL_WIDTH, D_MODEL), const),
            pl.BlockSpec((ATTN_WIDTH, D_MODEL), const),
            pl.BlockSpec((D_MODEL, D_MODEL), const),
        ],
        out_specs=pl.BlockSpec((TM, D_MODEL), row),
        out_shape=jax.ShapeDtypeStruct((t, D_MODEL), F32),
        scratch_shapes=[pltpu.VMEM((POOL_HALO, POOL_WIDTH), F32)],
        compiler_params=pltpu.CompilerParams(
            dimension_semantics=("arbitrary",), vmem_limit_bytes=VMEM_LIMIT),
        name="pool_merge",
    )(x2, u_pool, attn, mg, pool_w, pool_scale, w_pool_br, w_attn_br, w_o)


def _ffn_kernel(x_ref, nw_ref, wup_ref, cw_ref, cb_ref, wdn_ref, o_ref, halo_ref, act_ref, *, tiles_per_seq):
    i = pl.program_id(0)
    tm = x_ref.shape[0]
    x = x_ref[...]
    ms = jnp.mean(x * x, axis=-1, keepdims=True)
    h = ((x * lax.rsqrt(ms + EPS)) * nw_ref[...]).astype(BF16)

    @pl.when((i % tiles_per_seq) == 0)
    def _seq_start():
        halo_ref[...] = jnp.zeros_like(halo_ref)

    def conv(c):
        u = _dot(h, wup_ref[c])
        halo = halo_ref[c]
        halo_ref[c] = u[tm - CONV_HALO:, :]
        ue = jnp.concatenate([halo, u], axis=0)
        w = cw_ref[c]
        u1 = pltpu.roll(ue, 1, 0)[CONV_HALO:, :]
        u2 = pltpu.roll(ue, 2, 0)[CONV_HALO:, :]
        return cb_ref[c] + w[0:1, :] * u2 + w[1:2, :] * u1 + w[2:3, :] * u

    for c in range(N_FF_CHUNKS):
        gate = conv(c)
        val = conv(N_FF_CHUNKS + c)
        act = (gate * jax.nn.sigmoid(gate)) * val
        act_ref[:, c * FF_CHUNK:(c + 1) * FF_CHUNK] = act.astype(BF16)
    o_ref[...] = x + _dot(act_ref[...], wdn_ref[...])


def _ffn(x1, nw, wup, cw, cb, wdn, seq):
    t = x1.shape[0]
    row = lambda i: (i, 0)
    c3 = lambda i: (0, 0, 0)
    return pl.pallas_call(
        functools.partial(_ffn_kernel, tiles_per_seq=seq // TM),
        grid=(t // TM,),
        in_specs=[
            pl.BlockSpec((TM, D_MODEL), row),
            pl.BlockSpec((1, D_MODEL), lambda i: (0, 0)),
            pl.BlockSpec((2 * N_FF_CHUNKS, D_MODEL, FF_CHUNK), c3),
            pl.BlockSpec((2 * N_FF_CHUNKS, CONV_HALO, FF_CHUNK), c3),
            pl.BlockSpec((2 * N_FF_CHUNKS, 1, FF_CHUNK), c3),
            pl.BlockSpec((D_FF_PAD, D_MODEL), lambda i: (0, 0)),
        ],
        out_specs=pl.BlockSpec((TM, D_MODEL), row),
        out_shape=jax.ShapeDtypeStruct((t, D_MODEL), F32),
        scratch_shapes=[pltpu.VMEM((2 * N_FF_CHUNKS, CONV_HALO, FF_CHUNK), F32),
                        pltpu.VMEM((TM, D_FF_PAD), BF16)],
        compiler_params=pltpu.CompilerParams(
            dimension_semantics=("arbitrary",), vmem_limit_bytes=VMEM_LIMIT),
        name="conv_ffn",
    )(x1, nw, wup, cw, cb, wdn)


def _w_in_columns():
    ranges = [(0, OFF_KV)]
    for kv in range(2):
        for br in range(N_BRANCH):
            base = OFF_KV + br * 4 * HEAD_DIM + kv * 2 * HEAD_DIM
            ranges.append((base, base + 2 * HEAD_DIM))
    ranges.append((OFF_MERGE_G, OFF_MERGE_G + 2 * D_MODEL))
    gate_cols = np.full((LANES,), -1, np.int64)
    for h in range(N_HEADS):
        for br in range(N_BRANCH):
            lane = _gate_lane(h // HEADS_PER_GROUP, (h % HEADS_PER_GROUP) // 2, h % 2, br)
            gate_cols[lane] = h * N_BRANCH + br
    return ranges, gate_cols


def _rope_tables(seq):
    half = HEAD_DIM // 2
    freqs = ROPE_THETA ** (-jnp.arange(half, dtype=F32) / half)
    ang = jnp.arange(seq, dtype=F32)[:, None] * freqs[None, :]
    cos = jnp.cos(ang)
    sin = jnp.sin(ang)
    zero = jnp.zeros_like(sin)
    cos_t = jnp.tile(cos, (1, LANES // half))
    sin_a = jnp.tile(jnp.concatenate([-sin, zero], axis=1), (1, LANES // HEAD_DIM))
    sin_b = jnp.tile(jnp.concatenate([zero, sin], axis=1), (1, LANES // HEAD_DIM))
    return cos_t, sin_a, sin_b


def _selection_constants(seq):
    n_cmp = (seq - CMP_BLOCK) // CMP_STRIDE + 1
    n_slc = seq // SLC_BLOCK
    ci = np.arange(N_CMP_PAD)[:, None]
    sj = np.arange(N_SLC_PAD)[None, :]
    ov = ((ci * CMP_STRIDE < (sj + 1) * SLC_BLOCK) & (ci * CMP_STRIDE + CMP_BLOCK > sj * SLC_BLOCK)
          & (ci < n_cmp) & (sj < n_slc))
    ovt = ov.T[:n_slc]
    eb = (np.arange(seq)[None, :] // SLC_BLOCK) == np.arange(N_SLC_PAD)[:, None]
    eb = eb.reshape(N_SLC_PAD, seq // TQ, TQ).transpose(1, 0, 2)
    gw = HEADS_PER_GROUP * HEAD_DIM
    e = np.zeros((N_KV_GROUPS, N_BRANCH, 2 * LANES, gw), np.float32)
    for g in range(N_KV_GROUPS):
        for br in range(N_BRANCH):
            for h in range(HEADS_PER_GROUP):
                lane = _gate_lane(g, h // 2, h % 2, br)
                e[g, br, lane, h * HEAD_DIM:(h + 1) * HEAD_DIM] = 1.0
                e[g, br, LANES + lane, h * HEAD_DIM:(h + 1) * HEAD_DIM] = 1.0
    return (jnp.asarray(ovt, BF16), jnp.asarray(eb, BF16), jnp.asarray(e, BF16))


def _chunk_cols(w, n_chunks):
    return jnp.moveaxis(w.reshape(w.shape[:-1] + (n_chunks, FF_CHUNK)), -2, 0)


def _layer(x, attn_norm_w, w_in, pool_w, pool_scale, q_norm_w, k_norm_w, cmp_pos, cmp_w1, cmp_b1, cmp_w2,
           w_pool_br, w_attn_br, w_o, ffn_norm_w, w_up, conv_w, conv_b, w_down):
    batch, seq, _ = x.shape
    t = batch * seq
    x2 = x.reshape(t, D_MODEL)

    col_ranges, gate_cols = _w_in_columns()
    w_gate = jnp.where(gate_cols[None, :] >= 0,
                       jnp.take(w_in[:, OFF_NSA_G:OFF_MERGE_G], np.maximum(gate_cols, 0), axis=1), 0.0)
    w_perm = jnp.concatenate([w_in[:, a:b] for a, b in col_ranges] + [w_gate], axis=1).astype(BF16)
    normw = jnp.concatenate([jnp.tile(q_norm_w[None, :], (1, 2)), jnp.tile(k_norm_w, (1, 2))], axis=0)
    cos_t, sin_a, sin_b = _rope_tables(seq)
    u_pool, q, kvc, ks, kw, vs, vw, mg, gate = _inproj(
        x2, attn_norm_w[None, :], w_perm, normw, cos_t, sin_a, sin_b, seq)

    w1 = cmp_w1.reshape(2, CMP_BLOCK, HEAD_DIM, CMP_HIDDEN)
    zero = jnp.zeros_like(w1)
    w1bd = jnp.concatenate([jnp.concatenate([w1, zero], axis=-1), jnp.concatenate([zero, w1], axis=-1)],
                           axis=2).astype(BF16)
    pos128 = jnp.tile(cmp_pos, (1, 1, N_KV_GROUPS))
    b1 = jnp.tile(cmp_b1, (1, N_KV_GROUPS))[:, None, :]
    w2d = jnp.concatenate([cmp_w2, cmp_w2], axis=-1).astype(BF16)
    cmp_kv = _compress(kvc, pos128, w1bd, b1, w2d, batch, seq)

    ov_mat, eb_mat, e_mat = _selection_constants(seq)
    attn = _attention(q, ks, kw, vs, vw, cmp_kv, gate, e_mat, ov_mat, eb_mat, batch, seq)

    x1 = _merge(x2, u_pool, attn, mg, pool_w.astype(BF16), pool_scale[None, :], w_pool_br.astype(BF16),
                w_attn_br.astype(BF16), w_o.astype(BF16), seq)

    pad = D_FF_PAD - D_FF

    def pad_halves(w):
        lead = [(0, 0)] * (w.ndim - 1)
        return jnp.concatenate([jnp.pad(w[..., :D_FF], lead + [(0, pad)]),
                                jnp.pad(w[..., D_FF:], lead + [(0, pad)])], axis=-1)

    wup = _chunk_cols(pad_halves(w_up).astype(BF16), 2 * N_FF_CHUNKS)
    cw = _chunk_cols(jnp.pad(pad_halves(conv_w), ((0, CONV_HALO - CONV_WIDTH), (0, 0))), 2 * N_FF_CHUNKS)
    cb = _chunk_cols(pad_halves(conv_b)[None, :], 2 * N_FF_CHUNKS)
    wdn = jnp.pad(w_down, ((0, pad), (0, 0))).astype(BF16)
    out = _ffn(x1, ffn_norm_w[None, :], wup, cw, cb, wdn, seq)
    return out.reshape(batch, seq, D_MODEL)


def kernel(x, attn_norm_w, w_in, pool_w, pool_scale, q_norm_w, k_norm_w, cmp_pos, cmp_w1, cmp_b1, cmp_w2,
           w_pool_br, w_attn_br, w_o, ffn_norm_w, w_up, conv_w, conv_b, w_down):
    for l in range(attn_norm_w.shape[0]):
        x = _layer(x, attn_norm_w[l], w_in[l], pool_w[l], pool_scale[l], q_norm_w[l], k_norm_w[l],
                   cmp_pos[l], cmp_w1[l], cmp_b1[l], cmp_w2[l], w_pool_br[l], w_attn_br[l], w_o[l],
                   ffn_norm_w[l], w_up[l], conv_w[l], conv_b[l], w_down[l])
    return x
```

```python
import functools

import numpy as np
import jax
import jax.numpy as jnp
from jax import lax
from jax.experimental import pallas as pl
from jax.experimental.pallas import tpu as pltpu

D_MODEL = 1024
POOL_WINDOWS = (2, 4, 8, 16)
N_POOL_GROUPS = 4
POOL_GROUP = 128
POOL_WIDTH = 512
HEAD_DIM = 64
N_HEADS = 16
N_KV_GROUPS = 2
HEADS_PER_GROUP = 8
ATTN_WIDTH = 1024
N_BRANCH = 3
KV_WIDTH = 768
CMP_BLOCK = 32
CMP_STRIDE = 16
CMP_HIDDEN = 128
SLC_BLOCK = 64
SLC_TOP_N = 8
WINDOW = 512
ROPE_THETA = 10000.0
SCALE = HEAD_DIM ** -0.5
FORCE_BONUS = 1000.0
NEG_INF = -1e30
OFF_Q = POOL_WIDTH
OFF_KV = OFF_Q + ATTN_WIDTH
OFF_NSA_G = OFF_KV + KV_WIDTH
OFF_MERGE_G = OFF_NSA_G + N_BRANCH * N_HEADS
D_FF = 2752
CONV_WIDTH = 3
EPS = 1e-6
LOG2E = 1.4426950408889634

LANES = 128
MXU_N = 256
VMEM_LIMIT = 56 * 1024 * 1024

TM = 512
TQ = 256
SCORE_LOOKAHEAD = 5
D_FF_PAD = 2816
FF_CHUNK = 256
N_FF_CHUNKS = D_FF_PAD // FF_CHUNK
POOL_HALO = 16
CONV_HALO = 8
N_CMP_PAD = 128
N_SLC_PAD = 128

F32 = jnp.float32
BF16 = jnp.bfloat16


def _dot(a, b):
    return jnp.dot(a, b, preferred_element_type=F32)


def _dot_nt(a, b):
    return lax.dot_general(a, b, (((1,), (1,)), ((), ())), preferred_element_type=F32)


_SLAB_POOL = 0
_SLAB_Q = 4
_SLAB_K = 12
_SLAB_V = 15
_SLAB_MERGE = 18
_SLAB_GATE = 34
_N_SLABS = 35


def _inproj_kernel(x_ref, anw_ref, w_ref, nw_ref, cos_ref, sa_ref, sb_ref,
                   up_ref, q_ref, kvc_ref, ks_ref, kw_ref, vs_ref, vw_ref, mg_ref, g_ref):
    x = x_ref[...]
    ms = jnp.mean(x * x, axis=-1, keepdims=True)
    h = ((x * lax.rsqrt(ms + EPS)) * anw_ref[...]).astype(BF16)
    tm = x.shape[0]
    lane = lax.broadcasted_iota(jnp.int32, (tm, LANES), 1)
    lo = lane < HEAD_DIM
    cos = cos_ref[...]
    sa = sa_ref[...]
    sb = sb_ref[...]

    def norm_rope(z, nw, scale):
        ss = z * z
        s_lo = jnp.sum(jnp.where(lo, ss, 0.0), axis=-1, keepdims=True)
        s_hi = jnp.sum(jnp.where(lo, 0.0, ss), axis=-1, keepdims=True)
        msq = jnp.where(lo, s_lo, s_hi) * (1.0 / HEAD_DIM)
        y = (z * lax.rsqrt(msq + EPS)) * nw
        r = y * cos + pltpu.roll(y, LANES - HEAD_DIM // 2, 1) * sa + pltpu.roll(y, HEAD_DIM // 2, 1) * sb
        return r * scale if scale is not None else r

    k_refs = (None, ks_ref, kw_ref)
    v_refs = (None, vs_ref, vw_ref)

    def emit(slab, z):
        if slab < _SLAB_Q:
            c = (slab - _SLAB_POOL) * LANES
            up_ref[:, c:c + LANES] = z
        elif slab < _SLAB_K:
            c = (slab - _SLAB_Q) * LANES
            q_ref[:, c:c + LANES] = norm_rope(z, nw_ref[0:1, :], SCALE * LOG2E).astype(BF16)
        elif slab < _SLAB_V:
            br = slab - _SLAB_K
            r = norm_rope(z, nw_ref[1 + br:2 + br, :], None)
            if br == 0:
                kvc_ref[:, 0:LANES] = r
            else:
                k_refs[br][...] = r.astype(BF16)
        elif slab < _SLAB_MERGE:
            br = slab - _SLAB_V
            if br == 0:
                kvc_ref[:, LANES:2 * LANES] = z
            else:
                v_refs[br][...] = z.astype(BF16)
        elif slab < _SLAB_GATE:
            c = (slab - _SLAB_MERGE) * LANES
            mg_ref[:, c:c + LANES] = jax.nn.sigmoid(z).astype(BF16)
        else:
            g_ref[...] = jax.nn.sigmoid(z)

    slab = 0
    while slab < _N_SLABS:
        width = 2 if slab + 1 < _N_SLABS else 1
        z = _dot(h, w_ref[:, slab * LANES:(slab + width) * LANES])
        for j in range(width):
            emit(slab + j, z[:, j * LANES:(j + 1) * LANES])
        slab += width


def _inproj(x2, anw, w_perm, normw, cos_t, sin_a, sin_b, seq):
    t = x2.shape[0]
    tiles_per_seq = seq // TM
    row = lambda i: (i, 0)
    const = lambda i: (0, 0)
    pos = lambda i: (i % tiles_per_seq, 0)
    out_shapes = (
        jax.ShapeDtypeStruct((t, POOL_WIDTH), F32),
        jax.ShapeDtypeStruct((t, ATTN_WIDTH), BF16),
        jax.ShapeDtypeStruct((t, 2 * LANES), F32),
        jax.ShapeDtypeStruct((t, LANES), BF16),
        jax.ShapeDtypeStruct((t, LANES), BF16),
        jax.ShapeDtypeStruct((t, LANES), BF16),
        jax.ShapeDtypeStruct((t, LANES), BF16),
        jax.ShapeDtypeStruct((t, 2 * D_MODEL), BF16),
        jax.ShapeDtypeStruct((t, LANES), F32),
    )
    out_specs = tuple(pl.BlockSpec((TM, s.shape[1]), row) for s in out_shapes)
    return pl.pallas_call(
        _inproj_kernel,
        grid=(t // TM,),
        in_specs=[
            pl.BlockSpec((TM, D_MODEL), row),
            pl.BlockSpec((1, D_MODEL), const),
            pl.BlockSpec((D_MODEL, _N_SLABS * LANES), const),
            pl.BlockSpec((4, LANES), const),
            pl.BlockSpec((TM, LANES), pos),
            pl.BlockSpec((TM, LANES), pos),
            pl.BlockSpec((TM, LANES), pos),
        ],
        out_specs=out_specs,
        out_shape=out_shapes,
        compiler_params=pltpu.CompilerParams(
            dimension_semantics=("arbitrary",), vmem_limit_bytes=VMEM_LIMIT),
        name="inproj",
    )(x2, anw, w_perm, normw, cos_t, sin_a, sin_b)


def _compress_kernel(x_ref, pos_ref, w1_ref, b1_ref, w2_ref, w2t_ref, o_ref):
    n = x_ref.shape[0] // CMP_STRIDE
    top = None
    bot = None
    for l in range(CMP_STRIDE):
        y = x_ref[pl.ds(l, n, stride=CMP_STRIDE), :]
        t = _dot((y + pos_ref[0, l:l + 1, :]).astype(BF16), w1_ref[0, l])
        b = _dot((y + pos_ref[0, CMP_STRIDE + l:CMP_STRIDE + l + 1, :]).astype(BF16), w1_ref[0, CMP_STRIDE + l])
        top = t if top is None else top + t
        bot = b if bot is None else bot + b
    hid = top + pltpu.roll(bot, n - 1, 0) + b1_ref[0]
    act = jax.nn.gelu(hid).astype(BF16)
    @pl.when(pl.program_id(0) == 0)
    def _keys():
        for g in range(N_KV_GROUPS):
            o_ref[0, 0, g] = _dot(act[:, g * CMP_HIDDEN:(g + 1) * CMP_HIDDEN], w2_ref[0]).astype(o_ref.dtype)

    @pl.when(pl.program_id(0) == 1)
    def _values():
        for g in range(N_KV_GROUPS):
            o_ref[0, 0, g] = _dot_nt(w2t_ref[0], act[:, g * CMP_HIDDEN:(g + 1) * CMP_HIDDEN]).astype(o_ref.dtype)


def _compress(kvc, pos128, w1bd, b1, w2d, batch, seq):
    n = seq // CMP_STRIDE
    assert n == N_CMP_PAD == LANES
    w2t = jnp.swapaxes(w2d, 1, 2)
    return pl.pallas_call(
        _compress_kernel,
        grid=(2, batch),
        in_specs=[
            pl.BlockSpec((seq, LANES), lambda a, i: (i, a)),
            pl.BlockSpec((1, CMP_BLOCK, LANES), lambda a, i: (a, 0, 0)),
            pl.BlockSpec((1, CMP_BLOCK, LANES, N_KV_GROUPS * CMP_HIDDEN), lambda a, i: (a, 0, 0, 0)),
            pl.BlockSpec((1, 1, N_KV_GROUPS * CMP_HIDDEN), lambda a, i: (a, 0, 0)),
            pl.BlockSpec((1, CMP_HIDDEN, LANES), lambda a, i: (a, 0, 0)),
            pl.BlockSpec((1, LANES, CMP_HIDDEN), lambda a, i: (a, 0, 0)),
        ],
        out_specs=pl.BlockSpec((1, 1, N_KV_GROUPS, n, LANES), lambda a, i: (a, i, 0, 0, 0)),
        out_shape=jax.ShapeDtypeStruct((2, batch, N_KV_GROUPS, n, LANES), BF16),
        compiler_params=pltpu.CompilerParams(
            dimension_semantics=("arbitrary", "arbitrary"), vmem_limit_bytes=VMEM_LIMIT),
        name="compress",
    )(kvc, pos128, w1bd, b1, w2d, w2t)


def _attn_t_kernel(q_ref, ks_ref, kw_ref, vs_ref, vw_ref, ck_ref, cvt_ref, gate_ref, ovt_ref, ebt_ref,
                   o_ref, kks, kkw, vts, vtw, qst, sel_ref, m_s, m_w, acc_s, acc_w):
    g = pl.program_id(1)
    i = pl.program_id(2)
    seq = ks_ref.shape[0]
    nk = seq // TQ
    n_pairs = HEADS_PER_GROUP // 2

    @pl.when(i == 0)
    def _prep():
        lane = lax.broadcasted_iota(jnp.int32, (TQ, LANES), 1)
        mine = (lane >= HEAD_DIM).astype(jnp.int32) == g
        top = lax.broadcasted_iota(jnp.int32, (LANES, TQ), 0) < HEAD_DIM

        def both_halves(x):
            xr = jnp.concatenate([x[:, HEAD_DIM:], x[:, :HEAD_DIM]], axis=1)
            return jnp.where(mine, x, xr)

        for kt in range(nk):
            keys = slice(kt * TQ, (kt + 1) * TQ)
            kks[keys, :] = both_halves(ks_ref[keys, :])
            kkw[keys, :] = both_halves(kw_ref[keys, :])
            for src, dst in ((vs_ref, vts), (vw_ref, vtw)):
                vt = both_halves(src[keys, :]).astype(F32).T
                dst[0, kt] = jnp.where(top, vt, 1.0).astype(BF16)
                dst[1, kt] = jnp.where(top, 1.0, vt).astype(BF16)

    lane_q = lax.broadcasted_iota(jnp.int32, (TQ, LANES), 1)
    lo_q = lane_q < HEAD_DIM
    for p in range(n_pairs):
        qp = q_ref[:, p * LANES:(p + 1) * LANES]
        zero = jnp.zeros_like(qp)
        qst[p * TQ:(p + 1) * TQ, :] = jnp.where(lo_q, qp, zero)
        qst[(n_pairs + p) * TQ:(n_pairs + p + 1) * TQ, :] = jnp.where(lo_q, zero, qp)

    def q_block(hb):
        return qst[hb * TQ:(hb + 1) * TQ, :]

    cmp_tok = lax.broadcasted_iota(jnp.int32, (N_CMP_PAD, TQ), 0)
    tq_c = i * TQ + lax.broadcasted_iota(jnp.int32, (N_CMP_PAD, TQ), 1)
    cmp_mask = (cmp_tok * CMP_STRIDE + (CMP_BLOCK - 1)) <= tq_c
    ck = ck_ref[0, 0, 0]
    cvt = cvt_ref[0, 0, 0]
    pcs = None
    o_c = []
    for hb in range(HEADS_PER_GROUP):
        s = jnp.where(cmp_mask, _dot_nt(ck, q_block(hb)), NEG_INF)
        m = jnp.max(s, axis=0, keepdims=True)
        e = jnp.where(cmp_mask, jnp.exp2(s - m), 0.0)
        l = jnp.sum(e, axis=0, keepdims=True)
        pc = e * jnp.where(l > 0.0, 1.0 / l, 0.0)
        pcs = pc if pcs is None else pcs + pc
        o_c.append(_dot(cvt, pc.astype(BF16)))

    pcs_hi = pcs.astype(BF16)
    pcs_lo = (pcs - pcs_hi.astype(F32)).astype(BF16)
    imp = _dot(ovt_ref[...], pcs_hi) + _dot(ovt_ref[...], pcs_lo)
    n_blk = seq // SLC_BLOCK
    blk = lax.broadcasted_iota(jnp.int32, (n_blk, TQ), 0)
    tq_t = i * TQ + lax.broadcasted_iota(jnp.int32, (n_blk, TQ), 1)
    cur = lax.shift_right_logical(tq_t, SLC_BLOCK.bit_length() - 1)
    forced = (blk == 0) | (blk == cur) | (blk == cur - 1)
    valid = blk * SLC_BLOCK <= tq_t
    score = jnp.where(valid, imp + jnp.where(forced, FORCE_BONUS, 0.0), NEG_INF)
    blk_f = blk.astype(F32)
    sel_t = jnp.zeros((n_blk, TQ), F32)
    for _ in range(SLC_TOP_N):
        mx = jnp.max(score, axis=0, keepdims=True)
        first = jnp.min(jnp.where(score == mx, blk_f, float(n_blk)), axis=0, keepdims=True)
        pick = blk_f == first
        sel_t = jnp.where(pick, 1.0, sel_t)
        score = jnp.where(pick, -jnp.inf, score)
    sel_ref[...] = jnp.concatenate([sel_t, jnp.zeros((N_SLC_PAD - n_blk, TQ), F32)], axis=0).astype(BF16)

    key_row = lax.broadcasted_iota(jnp.int32, (TQ, TQ), 0)
    q_col = lax.broadcasted_iota(jnp.int32, (TQ, TQ), 1)
    causal_bias = jnp.where(key_row <= q_col, 0.0, NEG_INF)
    tail_bias = jnp.where(key_row > q_col, 0.0, NEG_INF)

    def run_steps(steps):
        def score_stage(ctx, hb):
            k_tile, _, _, m_ref, bias, first = ctx
            s = _dot_nt(k_tile, q_block(hb))
            if bias is not None:
                s = s + bias
            m_new = jnp.max(s, axis=0, keepdims=True)
            if first:
                alpha = None
            else:
                m_old = m_ref[hb]
                m_new = jnp.maximum(m_old, m_new)
                alpha = jnp.exp2(m_old - m_new)
            m_ref[hb] = m_new
            return jnp.exp2(s - m_new).astype(BF16), alpha

        def value_stage(ctx, hb, p, alpha):
            _, vt_ref, acc_ref, _, _, _ = ctx
            kt = vt_ref[1]
            pv = _dot(vt_ref[0][hb // n_pairs, kt], p)
            acc_ref[hb] = pv if alpha is None else alpha * acc_ref[hb] + pv

        pending = []
        for k_ref, vt_ref, acc_ref, m_ref, kt, bias_fn, first in steps:
            keys = pl.ds(pl.multiple_of(kt * TQ, TQ), TQ)
            ctx = (k_ref[keys, :], (vt_ref, kt), acc_ref, m_ref, bias_fn(), first)
            for hb in range(HEADS_PER_GROUP):
                pending.append((ctx, hb) + score_stage(ctx, hb))
                if len(pending) > SCORE_LOOKAHEAD:
                    value_stage(*pending.pop(0))
        for item in pending:
            value_stage(*item)

    def sel_bias(kt, extra=None):
        b = (_dot(ebt_ref[kt], sel_ref[...]) - 1.0) * (-NEG_INF)
        return b if extra is None else b + extra

    def win(kt, bias, first=False):
        return (kkw, vtw, acc_w, m_w, kt, lambda: bias, first)

    def sel(kt, first=False):
        return (kks, vts, acc_s, m_s, kt, lambda: sel_bias(kt, causal_bias if first else None), first)

    run_steps([win(i, causal_bias, True), sel(i, True)])

    @pl.when(i == 1)
    def _w1():
        run_steps([win(0, None)])

    @pl.when(i >= 2)
    def _w2():
        run_steps([win(i - 1, None), win(i - 2, tail_bias)])

    def sel_pair(j, carry):
        run_steps([sel(2 * j), sel(2 * j + 1)])
        return carry

    lax.fori_loop(0, lax.shift_right_logical(i, 1), sel_pair, 0)

    @pl.when((i & 1) == 1)
    def _sel_odd():
        run_steps([sel(i - 1)])

    gate_t = gate_ref[...].T

    def gate_row(head, br):
        r0 = head * N_BRANCH + br
        r1 = (HEADS_PER_GROUP + head) * N_BRANCH + br
        return jnp.where(g == 0, gate_t[r0:r0 + 1, :], gate_t[r1:r1 + 1, :])

    outs = []
    for head in range(HEADS_PER_GROUP):
        parity = head % 2
        hb = parity * n_pairs + head // 2
        o_rows = slice(parity * HEAD_DIM, (parity + 1) * HEAD_DIM)
        l_row = (1 - parity) * HEAD_DIM
        t = gate_row(head, 0) * o_c[hb][0:HEAD_DIM, :]
        for br, acc_ref in ((1, acc_s), (2, acc_w)):
            acc = acc_ref[hb]
            t = t + (gate_row(head, br) * (1.0 / acc[l_row:l_row + 1, :])) * acc[o_rows, :]
        outs.append(t)
    o_ref[...] = jnp.concatenate(outs, axis=0).T.astype(o_ref.dtype)


def _attention_t(q, ks, kw, vs, vw, cmp_kv, gate, ovt_mat, ebt_mat, batch, seq):
    nq = seq // TQ
    gw = HEADS_PER_GROUP * HEAD_DIM
    kv_spec = pl.BlockSpec((seq, LANES), lambda b, g, i: (b, 0))
    return pl.pallas_call(
        _attn_t_kernel,
        grid=(batch, N_KV_GROUPS, nq),
        in_specs=[
            pl.BlockSpec((TQ, gw), lambda b, g, i: (b * nq + i, g)),
            kv_spec, kv_spec, kv_spec, kv_spec,
            pl.BlockSpec((1, 1, 1, N_CMP_PAD, LANES), lambda b, g, i: (0, b, g, 0, 0)),
            pl.BlockSpec((1, 1, 1, LANES, N_CMP_PAD), lambda b, g, i: (1, b, g, 0, 0)),
            pl.BlockSpec((TQ, LANES), lambda b, g, i: (b * nq + i, 0)),
            pl.BlockSpec((seq // SLC_BLOCK, N_CMP_PAD), lambda b, g, i: (0, 0)),
            pl.BlockSpec((nq, TQ, N_SLC_PAD), lambda b, g, i: (0, 0, 0)),
        ],
        out_specs=pl.BlockSpec((TQ, gw), lambda b, g, i: (b * nq + i, g)),
        out_shape=jax.ShapeDtypeStruct((batch * seq, ATTN_WIDTH), BF16),
        scratch_shapes=[
            pltpu.VMEM((seq, LANES), BF16), pltpu.VMEM((seq, LANES), BF16),
            pltpu.VMEM((2, nq, LANES, TQ), BF16),
            pltpu.VMEM((2, nq, LANES, TQ), BF16),
            pltpu.VMEM((HEADS_PER_GROUP * TQ, LANES), BF16),
            pltpu.VMEM((N_SLC_PAD, TQ), BF16),
            pltpu.VMEM((HEADS_PER_GROUP, 1, TQ), F32),
            pltpu.VMEM((HEADS_PER_GROUP, 1, TQ), F32),
            pltpu.VMEM((HEADS_PER_GROUP, LANES, TQ), F32),
            pltpu.VMEM((HEADS_PER_GROUP, LANES, TQ), F32),
        ],
        compiler_params=pltpu.CompilerParams(
            dimension_semantics=("arbitrary", "arbitrary", "arbitrary"), vmem_limit_bytes=VMEM_LIMIT),
        name="nsa_attention",
    )(q, ks, kw, vs, vw, cmp_kv, cmp_kv, gate, ovt_mat, ebt_mat)


def _merge_kernel(x_ref, up_ref, at_ref, mg_ref, pw_ref, ps_ref, wpb_ref, wab_ref, wo_ref,
                  o_ref, halo_ref, *, tiles_per_seq):
    i = pl.program_id(0)
    tm = x_ref.shape[0]
    u = up_ref[...]

    @pl.when((i % tiles_per_seq) == 0)
    def _seq_start():
        halo_ref[...] = jnp.zeros_like(halo_ref)

    halo = halo_ref[...]
    halo_ref[...] = u[tm - POOL_HALO:, :]
    ue = jnp.concatenate([halo, u], axis=0)
    rowp = lax.broadcasted_iota(jnp.int32, (tm, POOL_GROUP), 0) + (i % tiles_per_seq) * tm
    ys = []
    for gi, w in enumerate(POOL_WINDOWS):
        xg = ue[:, gi * POOL_GROUP:(gi + 1) * POOL_GROUP]
        sw = xg
        k = 1
        while k < w:
            sw = sw + pltpu.roll(sw, k, 0)
            k *= 2
        cnt = jnp.minimum(rowp + 1, w).astype(F32)
        ug = u[:, gi * POOL_GROUP:(gi + 1) * POOL_GROUP]
        pooled = sw[POOL_HALO:, :] / cnt - ug
        ys.append(_dot(pooled.astype(BF16), pw_ref[gi]))
    y = jnp.concatenate(ys, axis=1) * ps_ref[...]
    y_pool = _dot(y.astype(BF16), wpb_ref[...])
    y_attn = _dot(at_ref[...], wab_ref[...])
    mg = mg_ref[...]
    merged = mg[:, :D_MODEL] * y_pool + mg[:, D_MODEL:] * y_attn
    o_ref[...] = x_ref[...] + _dot(merged.astype(BF16), wo_ref[...])


def _merge(x2, u_pool, attn, mg, pool_w, pool_scale, w_pool_br, w_attn_br, w_o, seq):
    t = x2.shape[0]
    row = lambda i: (i, 0)
    const = lambda i: (0, 0)
    return pl.pallas_call(
        functools.partial(_merge_kernel, tiles_per_seq=seq // TM),
        grid=(t // TM,),
        in_specs=[
            pl.BlockSpec((TM, D_MODEL), row),
            pl.BlockSpec((TM, POOL_WIDTH), row),
            pl.BlockSpec((TM, ATTN_WIDTH), row),
            pl.BlockSpec((TM, 2 * D_MODEL), row),
            pl.BlockSpec((N_POOL_GROUPS, POOL_GROUP, POOL_GROUP), lambda i: (0, 0, 0)),
            pl.BlockSpec((1, POOL_WIDTH), const),
            pl.BlockSpec((POOL_WIDTH, D_MODEL), const),
            pl.BlockSpec((ATTN_WIDTH, D_MODEL), const),
            pl.BlockSpec((D_MODEL, D_MODEL), const),
        ],
        out_specs=pl.BlockSpec((TM, D_MODEL), row),
        out_shape=jax.ShapeDtypeStruct((t, D_MODEL), F32),
        scratch_shapes=[pltpu.VMEM((POOL_HALO, POOL_WIDTH), F32)],
        compiler_params=pltpu.CompilerParams(
            dimension_semantics=("arbitrary",), vmem_limit_bytes=VMEM_LIMIT),
        name="pool_merge",
    )(x2, u_pool, attn, mg, pool_w, pool_scale, w_pool_br, w_attn_br, w_o)


def _ffn_kernel(x_ref, nw_ref, wup_ref, cw_ref, cb_ref, wdn_ref, o_ref, halo_ref, act_ref, *, tiles_per_seq):
    i = pl.program_id(0)
    tm = x_ref.shape[0]
    x = x_ref[...]
    ms = jnp.mean(x * x, axis=-1, keepdims=True)
    h = ((x * lax.rsqrt(ms + EPS)) * nw_ref[...]).astype(BF16)

    @pl.when((i % tiles_per_seq) == 0)
    def _seq_start():
        halo_ref[...] = jnp.zeros_like(halo_ref)

    def conv(c):
        u = _dot(h, wup_ref[c])
        halo = halo_ref[c]
        halo_ref[c] = u[tm - CONV_HALO:, :]
        ue = jnp.concatenate([halo, u], axis=0)
        w = cw_ref[c]
        u1 = pltpu.roll(ue, 1, 0)[CONV_HALO:, :]
        u2 = pltpu.roll(ue, 2, 0)[CONV_HALO:, :]
        return cb_ref[c] + w[0:1, :] * u2 + w[1:2, :] * u1 + w[2:3, :] * u

    for c in range(N_FF_CHUNKS):
        gate = conv(c)
        val = conv(N_FF_CHUNKS + c)
        act = (gate * jax.nn.sigmoid(gate)) * val
        act_ref[:, c * FF_CHUNK:(c + 1) * FF_CHUNK] = act.astype(BF16)
    o_ref[...] = x + _dot(act_ref[...], wdn_ref[...])


def _ffn(x1, nw, wup, cw, cb, wdn, seq):
    t = x1.shape[0]
    row = lambda i: (i, 0)
    c3 = lambda i: (0, 0, 0)
    return pl.pallas_call(
        functools.partial(_ffn_kernel, tiles_per_seq=seq // TM),
        grid=(t // TM,),
        in_specs=[
            pl.BlockSpec((TM, D_MODEL), row),
            pl.BlockSpec((1, D_MODEL), lambda i: (0, 0)),
            pl.BlockSpec((2 * N_FF_CHUNKS, D_MODEL, FF_CHUNK), c3),
            pl.BlockSpec((2 * N_FF_CHUNKS, CONV_HALO, FF_CHUNK), c3),
            pl.BlockSpec((2 * N_FF_CHUNKS, 1, FF_CHUNK), c3),
            pl.BlockSpec((D_FF_PAD, D_MODEL), lambda i: (0, 0)),
        ],
        out_specs=pl.BlockSpec((TM, D_MODEL), row),
        out_shape=jax.ShapeDtypeStruct((t, D_MODEL), F32),
        scratch_shapes=[pltpu.VMEM((2 * N_FF_CHUNKS, CONV_HALO, FF_CHUNK), F32),
                        pltpu.VMEM((TM, D_FF_PAD), BF16)],
        compiler_params=pltpu.CompilerParams(
            dimension_semantics=("arbitrary",), vmem_limit_bytes=VMEM_LIMIT),
        name="conv_ffn",
    )(x1, nw, wup, cw, cb, wdn)


def _w_in_columns():
    ranges = [(0, OFF_KV)]
    for kv in range(2):
        for br in range(N_BRANCH):
            base = OFF_KV + br * 4 * HEAD_DIM + kv * 2 * HEAD_DIM
            ranges.append((base, base + 2 * HEAD_DIM))
    ranges.append((OFF_MERGE_G, OFF_MERGE_G + 2 * D_MODEL))
    ranges.append((OFF_NSA_G, OFF_MERGE_G))
    return ranges


def _rope_tables(seq):
    half = HEAD_DIM // 2
    freqs = ROPE_THETA ** (-jnp.arange(half, dtype=F32) / half)
    ang = jnp.arange(seq, dtype=F32)[:, None] * freqs[None, :]
    cos = jnp.cos(ang)
    sin = jnp.sin(ang)
    zero = jnp.zeros_like(sin)
    cos_t = jnp.tile(cos, (1, LANES // half))
    sin_a = jnp.tile(jnp.concatenate([-sin, zero], axis=1), (1, LANES // HEAD_DIM))
    sin_b = jnp.tile(jnp.concatenate([zero, sin], axis=1), (1, LANES // HEAD_DIM))
    return cos_t, sin_a, sin_b


def _selection_constants(seq):
    n_cmp = (seq - CMP_BLOCK) // CMP_STRIDE + 1
    n_slc = seq // SLC_BLOCK
    ci = np.arange(N_CMP_PAD)[:, None]
    sj = np.arange(N_SLC_PAD)[None, :]
    ov = ((ci * CMP_STRIDE < (sj + 1) * SLC_BLOCK) & (ci * CMP_STRIDE + CMP_BLOCK > sj * SLC_BLOCK)
          & (ci < n_cmp) & (sj < n_slc))
    ovt = ov.T[:n_slc]
    ebt = (np.arange(seq)[:, None] // SLC_BLOCK) == np.arange(N_SLC_PAD)[None, :]
    ebt = ebt.reshape(seq // TQ, TQ, N_SLC_PAD)
    return jnp.asarray(ovt, BF16), jnp.asarray(ebt, BF16)


def _chunk_cols(w, n_chunks):
    return jnp.moveaxis(w.reshape(w.shape[:-1] + (n_chunks, FF_CHUNK)), -2, 0)


def _layer(x, attn_norm_w, w_in, pool_w, pool_scale, q_norm_w, k_norm_w, cmp_pos, cmp_w1, cmp_b1, cmp_w2,
           w_pool_br, w_attn_br, w_o, ffn_norm_w, w_up, conv_w, conv_b, w_down):
    batch, seq, _ = x.shape
    t = batch * seq
    x2 = x.reshape(t, D_MODEL)

    w_perm = jnp.concatenate([w_in[:, a:b] for a, b in _w_in_columns()], axis=1)
    w_perm = jnp.pad(w_perm, ((0, 0), (0, _N_SLABS * LANES - w_perm.shape[1]))).astype(BF16)
    normw = jnp.concatenate([jnp.tile(q_norm_w[None, :], (1, 2)), jnp.tile(k_norm_w, (1, 2))], axis=0)
    cos_t, sin_a, sin_b = _rope_tables(seq)
    u_pool, q, kvc, ks, kw, vs, vw, mg, gate = _inproj(
        x2, attn_norm_w[None, :], w_perm, normw, cos_t, sin_a, sin_b, seq)

    w1 = cmp_w1.reshape(2, CMP_BLOCK, HEAD_DIM, CMP_HIDDEN)
    zero = jnp.zeros_like(w1)
    w1bd = jnp.concatenate([jnp.concatenate([w1, zero], axis=-1), jnp.concatenate([zero, w1], axis=-1)],
                           axis=2).astype(BF16)
    pos128 = jnp.tile(cmp_pos, (1, 1, N_KV_GROUPS))
    b1 = jnp.tile(cmp_b1, (1, N_KV_GROUPS))[:, None, :]
    w2d = jnp.concatenate([cmp_w2, cmp_w2], axis=-1).astype(BF16)
    cmp_kv = _compress(kvc, pos128, w1bd, b1, w2d, batch, seq)

    ovt_mat, ebt_mat = _selection_constants(seq)
    attn = _attention_t(q, ks, kw, vs, vw, cmp_kv, gate, ovt_mat, ebt_mat, batch, seq)

    x1 = _merge(x2, u_pool, attn, mg, pool_w.astype(BF16), pool_scale[None, :], w_pool_br.astype(BF16),
                w_attn_br.astype(BF16), w_o.astype(BF16), seq)

    pad = D_FF_PAD - D_FF

    def pad_halves(w):
        lead = [(0, 0)] * (w.ndim - 1)
        return jnp.concatenate([jnp.pad(w[..., :D_FF], lead + [(0, pad)]),
                                jnp.pad(w[..., D_FF:], lead + [(0, pad)])], axis=-1)

    wup = _chunk_cols(pad_halves(w_up).astype(BF16), 2 * N_FF_CHUNKS)
    cw = _chunk_cols(jnp.pad(pad_halves(conv_w), ((0, CONV_HALO - CONV_WIDTH), (0, 0))), 2 * N_FF_CHUNKS)
    cb = _chunk_cols(pad_halves(conv_b)[None, :], 2 * N_FF_CHUNKS)
    wdn = jnp.pad(w_down, ((0, pad), (0, 0))).astype(BF16)
    out = _ffn(x1, ffn_norm_w[None, :], wup, cw, cb, wdn, seq)
    return out.reshape(batch, seq, D_MODEL)


def kernel(x, attn_norm_w, w_in, pool_w, pool_scale, q_norm_w, k_norm_w, cmp_pos, cmp_w1, cmp_b1, cmp_w2,
           w_pool_br, w_attn_br, w_o, ffn_norm_w, w_up, conv_w, conv_b, w_down):
    for l in range(attn_norm_w.shape[0]):
        x = _layer(x, attn_norm_w[l], w_in[l], pool_w[l], pool_scale[l], q_norm_w[l], k_norm_w[l],
                   cmp_pos[l], cmp_w1[l], cmp_b1[l], cmp_w2[l], w_pool_br[l], w_attn_br[l], w_o[l],
                   ffn_norm_w[l], w_up[l], conv_w[l], conv_b[l], w_down[l])
    return x
```

```python
import functools

import numpy as np
import jax
import jax.numpy as jnp
from jax import lax
from jax.experimental import pallas as pl
from jax.experimental.pallas import tpu as pltpu

D_MODEL = 1024
POOL_WINDOWS = (2, 4, 8, 16)
N_POOL_GROUPS = 4
POOL_GROUP = 128
POOL_WIDTH = 512
HEAD_DIM = 64
N_HEADS = 16
N_KV_GROUPS = 2
HEADS_PER_GROUP = 8
ATTN_WIDTH = 1024
N_BRANCH = 3
KV_WIDTH = 768
CMP_BLOCK = 32
CMP_STRIDE = 16
CMP_HIDDEN = 128
SLC_BLOCK = 64
SLC_TOP_N = 8
WINDOW = 512
ROPE_THETA = 10000.0
SCALE = HEAD_DIM ** -0.5
FORCE_BONUS = 1000.0
NEG_INF = -1e30
OFF_Q = POOL_WIDTH
OFF_KV = OFF_Q + ATTN_WIDTH
OFF_NSA_G = OFF_KV + KV_WIDTH
OFF_MERGE_G = OFF_NSA_G + N_BRANCH * N_HEADS
D_FF = 2752
CONV_WIDTH = 3
EPS = 1e-6
LOG2E = 1.4426950408889634

LANES = 128
MXU_N = 256
VMEM_LIMIT = 56 * 1024 * 1024

TM = 512
TQ = 256
SCORE_LOOKAHEAD = 5
VT_ROWS = 80
D_FF_PAD = 2816
FF_CHUNK = 256
N_FF_CHUNKS = D_FF_PAD // FF_CHUNK
POOL_HALO = 16
CONV_HALO = 8
N_CMP_PAD = 128
N_SLC_PAD = 128

F32 = jnp.float32
BF16 = jnp.bfloat16


def _dot(a, b):
    return jnp.dot(a, b, preferred_element_type=F32)


def _dot_nt(a, b):
    return lax.dot_general(a, b, (((1,), (1,)), ((), ())), preferred_element_type=F32)


_SLAB_POOL = 0
_SLAB_Q = 4
_SLAB_K = 12
_SLAB_V = 15
_SLAB_MERGE = 18
_SLAB_GATE = 34
_N_SLABS = 35


def _inproj_kernel(x_ref, anw_ref, w_ref, nw_ref, cos_ref, sa_ref, sb_ref,
                   up_ref, q_ref, kvc_ref, ks_ref, kw_ref, vs_ref, vw_ref, mg_ref, g_ref):
    x = x_ref[...]
    ms = jnp.mean(x * x, axis=-1, keepdims=True)
    h = ((x * lax.rsqrt(ms + EPS)) * anw_ref[...]).astype(BF16)
    tm = x.shape[0]
    lane = lax.broadcasted_iota(jnp.int32, (tm, LANES), 1)
    lo = lane < HEAD_DIM
    cos = cos_ref[...]
    sa = sa_ref[...]
    sb = sb_ref[...]

    def norm_rope(z, nw, scale):
        ss = z * z
        s_lo = jnp.sum(jnp.where(lo, ss, 0.0), axis=-1, keepdims=True)
        s_hi = jnp.sum(jnp.where(lo, 0.0, ss), axis=-1, keepdims=True)
        msq = jnp.where(lo, s_lo, s_hi) * (1.0 / HEAD_DIM)
        y = (z * lax.rsqrt(msq + EPS)) * nw
        r = y * cos + pltpu.roll(y, LANES - HEAD_DIM // 2, 1) * sa + pltpu.roll(y, HEAD_DIM // 2, 1) * sb
        return r * scale if scale is not None else r

    k_refs = (None, ks_ref, kw_ref)
    v_refs = (None, vs_ref, vw_ref)

    def emit(slab, z):
        if slab < _SLAB_Q:
            c = (slab - _SLAB_POOL) * LANES
            up_ref[:, c:c + LANES] = z
        elif slab < _SLAB_K:
            c = (slab - _SLAB_Q) * LANES
            q_ref[:, c:c + LANES] = norm_rope(z, nw_ref[0:1, :], SCALE * LOG2E).astype(BF16)
        elif slab < _SLAB_V:
            br = slab - _SLAB_K
            r = norm_rope(z, nw_ref[1 + br:2 + br, :], None)
            if br == 0:
                kvc_ref[:, 0:LANES] = r
            else:
                k_refs[br][...] = r.astype(BF16)
        elif slab < _SLAB_MERGE:
            br = slab - _SLAB_V
            if br == 0:
                kvc_ref[:, LANES:2 * LANES] = z
            else:
                v_refs[br][...] = z.astype(BF16)
        elif slab < _SLAB_GATE:
            c = (slab - _SLAB_MERGE) * LANES
            mg_ref[:, c:c + LANES] = jax.nn.sigmoid(z).astype(BF16)
        else:
            g_ref[...] = jax.nn.sigmoid(z)

    slab = 0
    while slab < _N_SLABS:
        width = 2 if slab + 1 < _N_SLABS else 1
        z = _dot(h, w_ref[:, slab * LANES:(slab + width) * LANES])
        for j in range(width):
            emit(slab + j, z[:, j * LANES:(j + 1) * LANES])
        slab += width


def _inproj(x2, anw, w_perm, normw, cos_t, sin_a, sin_b, seq):
    t = x2.shape[0]
    tiles_per_seq = seq // TM
    row = lambda i: (i, 0)
    const = lambda i: (0, 0)
    pos = lambda i: (i % tiles_per_seq, 0)
    out_shapes = (
        jax.ShapeDtypeStruct((t, POOL_WIDTH), F32),
        jax.ShapeDtypeStruct((t, ATTN_WIDTH), BF16),
        jax.ShapeDtypeStruct((t, 2 * LANES), F32),
        jax.ShapeDtypeStruct((t, LANES), BF16),
        jax.ShapeDtypeStruct((t, LANES), BF16),
        jax.ShapeDtypeStruct((t, LANES), BF16),
        jax.ShapeDtypeStruct((t, LANES), BF16),
        jax.ShapeDtypeStruct((t, 2 * D_MODEL), BF16),
        jax.ShapeDtypeStruct((t, LANES), F32),
    )
    out_specs = tuple(pl.BlockSpec((TM, s.shape[1]), row) for s in out_shapes)
    return pl.pallas_call(
        _inproj_kernel,
        grid=(t // TM,),
        in_specs=[
            pl.BlockSpec((TM, D_MODEL), row),
            pl.BlockSpec((1, D_MODEL), const),
            pl.BlockSpec((D_MODEL, _N_SLABS * LANES), const),
            pl.BlockSpec((4, LANES), const),
            pl.BlockSpec((TM, LANES), pos),
            pl.BlockSpec((TM, LANES), pos),
            pl.BlockSpec((TM, LANES), pos),
        ],
        out_specs=out_specs,
        out_shape=out_shapes,
        compiler_params=pltpu.CompilerParams(
            dimension_semantics=("arbitrary",), vmem_limit_bytes=VMEM_LIMIT),
        name="inproj",
    )(x2, anw, w_perm, normw, cos_t, sin_a, sin_b)


def _compress_kernel(x_ref, pos_ref, w1_ref, b1_ref, w2_ref, w2t_ref, o_ref):
    n = x_ref.shape[0] // CMP_STRIDE
    top = None
    bot = None
    for l in range(CMP_STRIDE):
        y = x_ref[pl.ds(l, n, stride=CMP_STRIDE), :]
        t = _dot((y + pos_ref[0, l:l + 1, :]).astype(BF16), w1_ref[0, l])
        b = _dot((y + pos_ref[0, CMP_STRIDE + l:CMP_STRIDE + l + 1, :]).astype(BF16), w1_ref[0, CMP_STRIDE + l])
        top = t if top is None else top + t
        bot = b if bot is None else bot + b
    hid = top + pltpu.roll(bot, n - 1, 0) + b1_ref[0]
    act = jax.nn.gelu(hid).astype(BF16)
    @pl.when(pl.program_id(0) == 0)
    def _keys():
        for g in range(N_KV_GROUPS):
            o_ref[0, 0, g] = _dot(act[:, g * CMP_HIDDEN:(g + 1) * CMP_HIDDEN], w2_ref[0]).astype(o_ref.dtype)

    @pl.when(pl.program_id(0) == 1)
    def _values():
        for g in range(N_KV_GROUPS):
            o_ref[0, 0, g] = _dot_nt(w2t_ref[0], act[:, g * CMP_HIDDEN:(g + 1) * CMP_HIDDEN]).astype(o_ref.dtype)


def _compress(kvc, pos128, w1bd, b1, w2d, batch, seq):
    n = seq // CMP_STRIDE
    assert n == N_CMP_PAD == LANES
    w2t = jnp.swapaxes(w2d, 1, 2)
    return pl.pallas_call(
        _compress_kernel,
        grid=(2, batch),
        in_specs=[
            pl.BlockSpec((seq, LANES), lambda a, i: (i, a)),
            pl.BlockSpec((1, CMP_BLOCK, LANES), lambda a, i: (a, 0, 0)),
            pl.BlockSpec((1, CMP_BLOCK, LANES, N_KV_GROUPS * CMP_HIDDEN), lambda a, i: (a, 0, 0, 0)),
            pl.BlockSpec((1, 1, N_KV_GROUPS * CMP_HIDDEN), lambda a, i: (a, 0, 0)),
            pl.BlockSpec((1, CMP_HIDDEN, LANES), lambda a, i: (a, 0, 0)),
            pl.BlockSpec((1, LANES, CMP_HIDDEN), lambda a, i: (a, 0, 0)),
        ],
        out_specs=pl.BlockSpec((1, 1, N_KV_GROUPS, n, LANES), lambda a, i: (a, i, 0, 0, 0)),
        out_shape=jax.ShapeDtypeStruct((2, batch, N_KV_GROUPS, n, LANES), BF16),
        compiler_params=pltpu.CompilerParams(
            dimension_semantics=("arbitrary", "arbitrary"), vmem_limit_bytes=VMEM_LIMIT),
        name="compress",
    )(kvc, pos128, w1bd, b1, w2d, w2t)


def _attn_t_kernel(q_ref, ks_ref, kw_ref, vs_ref, vw_ref, ck_ref, cvt_ref, gate_ref, ovt_ref, ebt_ref,
                   o_ref, kks, kkw, vts, vtw, qst, sel_ref, oc_ref, m_s, m_w, acc_s, acc_w):
    g = pl.program_id(1)
    i = pl.program_id(2)
    seq = ks_ref.shape[0]
    nk = seq // TQ
    n_pairs = HEADS_PER_GROUP // 2

    @pl.when(i == 0)
    def _prep():
        lane = lax.broadcasted_iota(jnp.int32, (TQ, LANES), 1)
        mine = (lane >= HEAD_DIM).astype(jnp.int32) == g
        ones = jnp.ones((VT_ROWS - HEAD_DIM, TQ), F32)

        def both_halves(x):
            xr = jnp.concatenate([x[:, HEAD_DIM:], x[:, :HEAD_DIM]], axis=1)
            return jnp.where(mine, x, xr)

        for kt in range(nk):
            keys = slice(kt * TQ, (kt + 1) * TQ)
            kks[keys, :] = both_halves(ks_ref[keys, :])
            kkw[keys, :] = both_halves(kw_ref[keys, :])
            for src, dst in ((vs_ref, vts), (vw_ref, vtw)):
                vt = both_halves(src[keys, :]).astype(F32).T
                dst[kt] = jnp.concatenate([vt[:HEAD_DIM, :], ones], axis=0).astype(BF16)

    lane_q = lax.broadcasted_iota(jnp.int32, (TQ, LANES), 1)
    lo_q = lane_q < HEAD_DIM
    for p in range(n_pairs):
        qp = q_ref[:, p * LANES:(p + 1) * LANES]
        zero = jnp.zeros_like(qp)
        qst[p * TQ:(p + 1) * TQ, :] = jnp.where(lo_q, qp, zero)
        qst[(n_pairs + p) * TQ:(n_pairs + p + 1) * TQ, :] = jnp.where(lo_q, zero, qp)

    def q_block(hb):
        return qst[hb * TQ:(hb + 1) * TQ, :]

    cmp_tok = lax.broadcasted_iota(jnp.int32, (N_CMP_PAD, TQ), 0)
    tq_c = i * TQ + lax.broadcasted_iota(jnp.int32, (N_CMP_PAD, TQ), 1)
    cmp_mask = (cmp_tok * CMP_STRIDE + (CMP_BLOCK - 1)) <= tq_c
    ck = ck_ref[0, 0, 0]
    cvt = cvt_ref[0, 0, 0]
    def cmp_head(hb):
        s = jnp.where(cmp_mask, _dot_nt(ck, q_block(hb)), NEG_INF)
        m = jnp.max(s, axis=0, keepdims=True)
        e = jnp.where(cmp_mask, jnp.exp2(s - m), 0.0)
        l = jnp.sum(e, axis=0, keepdims=True)
        pc = e * jnp.where(l > 0.0, 1.0 / l, 0.0)
        oc_ref[hb] = _dot(cvt, pc.astype(BF16))[:HEAD_DIM, :]
        return pc

    def select_blocks():
        pcs = functools.reduce(lambda a, b: a + b, [cmp_head(hb) for hb in range(HEADS_PER_GROUP)])
        pcs_hi = pcs.astype(BF16)
        pcs_lo = (pcs - pcs_hi.astype(F32)).astype(BF16)
        imp = _dot(ovt_ref[...], pcs_hi) + _dot(ovt_ref[...], pcs_lo)
        n_blk = seq // SLC_BLOCK
        blk = lax.broadcasted_iota(jnp.int32, (n_blk, TQ), 0)
        tq_t = i * TQ + lax.broadcasted_iota(jnp.int32, (n_blk, TQ), 1)
        cur = lax.shift_right_logical(tq_t, SLC_BLOCK.bit_length() - 1)
        forced = (blk == 0) | (blk == cur) | (blk == cur - 1)
        valid = blk * SLC_BLOCK <= tq_t
        score = jnp.where(valid, imp + jnp.where(forced, FORCE_BONUS, 0.0), NEG_INF)
        blk_f = blk.astype(F32)
        sel_t = jnp.zeros((n_blk, TQ), F32)
        for _ in range(SLC_TOP_N):
            mx = jnp.max(score, axis=0, keepdims=True)
            first = jnp.min(jnp.where(score == mx, blk_f, float(n_blk)), axis=0, keepdims=True)
            pick = blk_f == first
            sel_t = jnp.where(pick, 1.0, sel_t)
            score = jnp.where(pick, -jnp.inf, score)
        sel_ref[...] = jnp.concatenate([sel_t, jnp.zeros((N_SLC_PAD - n_blk, TQ), F32)], axis=0).astype(BF16)

    key_row = lax.broadcasted_iota(jnp.int32, (TQ, TQ), 0)
    q_col = lax.broadcasted_iota(jnp.int32, (TQ, TQ), 1)
    causal_bias = jnp.where(key_row <= q_col, 0.0, NEG_INF)
    tail_bias = jnp.where(key_row > q_col, 0.0, NEG_INF)

    def run_steps(steps, interleave=()):
        interleave = list(interleave)
        def score_stage(ctx, hb):
            k_tile, _, _, m_ref, bias, first = ctx
            s = _dot_nt(k_tile, q_block(hb))
            if bias is not None:
                s = s + bias
            m_new = jnp.max(s, axis=0, keepdims=True)
            if first:
                alpha = None
            else:
                m_old = m_ref[hb]
                m_new = jnp.maximum(m_old, m_new)
                alpha = jnp.exp2(m_old - m_new)
            m_ref[hb] = m_new
            return jnp.exp2(s - m_new).astype(BF16), alpha

        def value_stage(ctx, hb, p, alpha):
            _, vt_ref, acc_ref, _, _, _ = ctx
            pv = _dot(vt_ref[0][vt_ref[1]], p)
            acc_ref[hb] = pv if alpha is None else alpha * acc_ref[hb] + pv

        pending = []
        for k_ref, vt_ref, acc_ref, m_ref, kt, bias_fn, first in steps:
            keys = pl.ds(kt * TQ, TQ)
            ctx = (k_ref[keys, :], (vt_ref, kt), acc_ref, m_ref, bias_fn(), first)
            for hb in range(HEADS_PER_GROUP):
                pending.append((ctx, hb) + score_stage(ctx, hb))
                if interleave:
                    interleave.pop(0)()
                if len(pending) > SCORE_LOOKAHEAD:
                    value_stage(*pending.pop(0))
        for item in pending:
            value_stage(*item)

    def sel_bias(kt, extra=None):
        b = (_dot(ebt_ref[kt], sel_ref[...]) - 1.0) * (-NEG_INF)
        return b if extra is None else b + extra

    def win(kt, bias, first=False):
        return (kkw, vtw, acc_w, m_w, kt, lambda: bias, first)

    def sel(kt):
        return (kks, vts, acc_s, m_s, kt, lambda: sel_bias(kt), False)

    def tile_program(qt):
        def first_sel_bias():
            select_blocks()
            return sel_bias(qt, causal_bias)

        steps = [win(qt, causal_bias, True)]
        if qt >= 1:
            steps.append(win(qt - 1, None))
        if qt >= 2:
            steps.append(win(qt - 2, tail_bias))
        steps.append((kks, vts, acc_s, m_s, qt, first_sel_bias, True))
        steps += [sel(kt) for kt in range(qt)]
        run_steps(steps)

    for qt in range(nk):
        pl.when(i == qt)(functools.partial(tile_program, qt))

    gate_t = gate_ref[...].T

    def gate_row(head, br):
        r0 = head * N_BRANCH + br
        r1 = (HEADS_PER_GROUP + head) * N_BRANCH + br
        return jnp.where(g == 0, gate_t[r0:r0 + 1, :], gate_t[r1:r1 + 1, :])

    outs = []
    for head in range(HEADS_PER_GROUP):
        hb = (head % 2) * n_pairs + head // 2
        t = gate_row(head, 0) * oc_ref[hb]
        for br, acc_ref in ((1, acc_s), (2, acc_w)):
            acc = acc_ref[hb]
            t = t + (gate_row(head, br) * (1.0 / acc[HEAD_DIM:HEAD_DIM + 1, :])) * acc[:HEAD_DIM, :]
        outs.append(t)
    o_ref[...] = jnp.concatenate(outs, axis=0).T.astype(o_ref.dtype)


def _attention_t(q, ks, kw, vs, vw, cmp_kv, gate, ovt_mat, ebt_mat, batch, seq):
    nq = seq // TQ
    gw = HEADS_PER_GROUP * HEAD_DIM
    kv_spec = pl.BlockSpec((seq, LANES), lambda b, g, i: (b, 0))
    return pl.pallas_call(
        _attn_t_kernel,
        grid=(batch, N_KV_GROUPS, nq),
        in_specs=[
            pl.BlockSpec((TQ, gw), lambda b, g, i: (b * nq + i, g)),
            kv_spec, kv_spec, kv_spec, kv_spec,
            pl.BlockSpec((1, 1, 1, N_CMP_PAD, LANES), lambda b, g, i: (0, b, g, 0, 0)),
            pl.BlockSpec((1, 1, 1, LANES, N_CMP_PAD), lambda b, g, i: (1, b, g, 0, 0)),
            pl.BlockSpec((TQ, LANES), lambda b, g, i: (b * nq + i, 0)),
            pl.BlockSpec((seq // SLC_BLOCK, N_CMP_PAD), lambda b, g, i: (0, 0)),
            pl.BlockSpec((nq, TQ, N_SLC_PAD), lambda b, g, i: (0, 0, 0)),
        ],
        out_specs=pl.BlockSpec((TQ, gw), lambda b, g, i: (b * nq + i, g)),
        out_shape=jax.ShapeDtypeStruct((batch * seq, ATTN_WIDTH), BF16),
        scratch_shapes=[
            pltpu.VMEM((seq, LANES), BF16), pltpu.VMEM((seq, LANES), BF16),
            pltpu.VMEM((nq, VT_ROWS, TQ), BF16), pltpu.VMEM((nq, VT_ROWS, TQ), BF16),
            pltpu.VMEM((HEADS_PER_GROUP * TQ, LANES), BF16),
            pltpu.VMEM((N_SLC_PAD, TQ), BF16),
            pltpu.VMEM((HEADS_PER_GROUP, HEAD_DIM, TQ), F32),
            pltpu.VMEM((HEADS_PER_GROUP, 1, TQ), F32),
            pltpu.VMEM((HEADS_PER_GROUP, 1, TQ), F32),
            pltpu.VMEM((HEADS_PER_GROUP, VT_ROWS, TQ), F32),
            pltpu.VMEM((HEADS_PER_GROUP, VT_ROWS, TQ), F32),
        ],
        compiler_params=pltpu.CompilerParams(
            dimension_semantics=("arbitrary", "arbitrary", "arbitrary"), vmem_limit_bytes=VMEM_LIMIT),
        name="nsa_attention",
    )(q, ks, kw, vs, vw, cmp_kv, cmp_kv, gate, ovt_mat, ebt_mat)


def _merge_kernel(x_ref, up_ref, at_ref, mg_ref, pw_ref, ps_ref, wpb_ref, wab_ref, wo_ref,
                  o_ref, halo_ref, *, tiles_per_seq):
    i = pl.program_id(0)
    tm = x_ref.shape[0]
    u = up_ref[...]

    @pl.when((i % tiles_per_seq) == 0)
    def _seq_start():
        halo_ref[...] = jnp.zeros_like(halo_ref)

    halo = halo_ref[...]
    halo_ref[...] = u[tm - POOL_HALO:, :]
    ue = jnp.concatenate([halo, u], axis=0)
    rowp = lax.broadcasted_iota(jnp.int32, (tm, POOL_GROUP), 0) + (i % tiles_per_seq) * tm
    ys = []
    for gi, w in enumerate(POOL_WINDOWS):
        xg = ue[:, gi * POOL_GROUP:(gi + 1) * POOL_GROUP]
        sw = xg
        k = 1
        while k < w:
            sw = sw + pltpu.roll(sw, k, 0)
            k *= 2
        cnt = jnp.minimum(rowp + 1, w).astype(F32)
        ug = u[:, gi * POOL_GROUP:(gi + 1) * POOL_GROUP]
        pooled = sw[POOL_HALO:, :] / cnt - ug
        ys.append(_dot(pooled.astype(BF16), pw_ref[gi]))
    y = jnp.concatenate(ys, axis=1) * ps_ref[...]
    y_pool = _dot(y.astype(BF16), wpb_ref[...])
    y_attn = _dot(at_ref[...], wab_ref[...])
    mg = mg_ref[...]
    merged = mg[:, :D_MODEL] * y_pool + mg[:, D_MODEL:] * y_attn
    o_ref[...] = x_ref[...] + _dot(merged.astype(BF16), wo_ref[...])


def _merge(x2, u_pool, attn, mg, pool_w, pool_scale, w_pool_br, w_attn_br, w_o, seq):
    t = x2.shape[0]
    row = lambda i: (i, 0)
    const = lambda i: (0, 0)
    return pl.pallas_call(
        functools.partial(_merge_kernel, tiles_per_seq=seq // TM),
        grid=(t // TM,),
        in_specs=[
            pl.BlockSpec((TM, D_MODEL), row),
            pl.BlockSpec((TM, POOL_WIDTH), row),
            pl.BlockSpec((TM, ATTN_WIDTH), row),
            pl.BlockSpec((TM, 2 * D_MODEL), row),
            pl.BlockSpec((N_POOL_GROUPS, POOL_GROUP, POOL_GROUP), lambda i: (0, 0, 0)),
            pl.BlockSpec((1, POOL_WIDTH), const),
            pl.BlockSpec((POOL_WIDTH, D_MODEL), const),
            pl.BlockSpec((ATTN_WIDTH, D_MODEL), const),
            pl.BlockSpec((D_MODEL, D_MODEL), const),
        ],
        out_specs=pl.BlockSpec((TM, D_MODEL), row),
        out_shape=jax.ShapeDtypeStruct((t, D_MODEL), F32),
        scratch_shapes=[pltpu.VMEM((POOL_HALO, POOL_WIDTH), F32)],
        compiler_params=pltpu.CompilerParams(
            dimension_semantics=("arbitrary",), vmem_limit_bytes=VMEM_LIMIT),
        name="pool_merge",
    )(x2, u_pool, attn, mg, pool_w, pool_scale, w_pool_br, w_attn_br, w_o)


def _ffn_kernel(x_ref, nw_ref, wup_ref, cw_ref, cb_ref, wdn_ref, o_ref, halo_ref, act_ref, *, tiles_per_seq):
    i = pl.program_id(0)
    tm = x_ref.shape[0]
    x = x_ref[...]
    ms = jnp.mean(x * x, axis=-1, keepdims=True)
    h = ((x * lax.rsqrt(ms + EPS)) * nw_ref[...]).astype(BF16)

    @pl.when((i % tiles_per_seq) == 0)
    def _seq_start():
        halo_ref[...] = jnp.zeros_like(halo_ref)

    def conv(c):
        u = _dot(h, wup_ref[c])
        halo = halo_ref[c]
        halo_ref[c] = u[tm - CONV_HALO:, :]
        ue = jnp.concatenate([halo, u], axis=0)
        w = cw_ref[c]
        u1 = pltpu.roll(ue, 1, 0)[CONV_HALO:, :]
        u2 = pltpu.roll(ue, 2, 0)[CONV_HALO:, :]
        return cb_ref[c] + w[0:1, :] * u2 + w[1:2, :] * u1 + w[2:3, :] * u

    for c in range(N_FF_CHUNKS):
        gate = conv(c)
        val = conv(N_FF_CHUNKS + c)
        act = (gate * jax.nn.sigmoid(gate)) * val
        act_ref[:, c * FF_CHUNK:(c + 1) * FF_CHUNK] = act.astype(BF16)
    o_ref[...] = x + _dot(act_ref[...], wdn_ref[...])


def _ffn(x1, nw, wup, cw, cb, wdn, seq):
    t = x1.shape[0]
    row = lambda i: (i, 0)
    c3 = lambda i: (0, 0, 0)
    return pl.pallas_call(
        functools.partial(_ffn_kernel, tiles_per_seq=seq // TM),
        grid=(t // TM,),
        in_specs=[
            pl.BlockSpec((TM, D_MODEL), row),
            pl.BlockSpec((1, D_MODEL), lambda i: (0, 0)),
            pl.BlockSpec((2 * N_FF_CHUNKS, D_MODEL, FF_CHUNK), c3),
            pl.BlockSpec((2 * N_FF_CHUNKS, CONV_HALO, FF_CHUNK), c3),
            pl.BlockSpec((2 * N_FF_CHUNKS, 1, FF_CHUNK), c3),
            pl.BlockSpec((D_FF_PAD, D_MODEL), lambda i: (0, 0)),
        ],
        out_specs=pl.BlockSpec((TM, D_MODEL), row),
        out_shape=jax.ShapeDtypeStruct((t, D_MODEL), F32),
        scratch_shapes=[pltpu.VMEM((2 * N_FF_CHUNKS, CONV_HALO, FF_CHUNK), F32),
                        pltpu.VMEM((TM, D_FF_PAD), BF16)],
        compiler_params=pltpu.CompilerParams(
            dimension_semantics=("arbitrary",), vmem_limit_bytes=VMEM_LIMIT),
        name="conv_ffn",
    )(x1, nw, wup, cw, cb, wdn)


def _w_in_columns():
    ranges = [(0, OFF_KV)]
    for kv in range(2):
        for br in range(N_BRANCH):
            base = OFF_KV + br * 4 * HEAD_DIM + kv * 2 * HEAD_DIM
            ranges.append((base, base + 2 * HEAD_DIM))
    ranges.append((OFF_MERGE_G, OFF_MERGE_G + 2 * D_MODEL))
    ranges.append((OFF_NSA_G, OFF_MERGE_G))
    return ranges


def _rope_tables(seq):
    half = HEAD_DIM // 2
    freqs = ROPE_THETA ** (-jnp.arange(half, dtype=F32) / half)
    ang = jnp.arange(seq, dtype=F32)[:, None] * freqs[None, :]
    cos = jnp.cos(ang)
    sin = jnp.sin(ang)
    zero = jnp.zeros_like(sin)
    cos_t = jnp.tile(cos, (1, LANES // half))
    sin_a = jnp.tile(jnp.concatenate([-sin, zero], axis=1), (1, LANES // HEAD_DIM))
    sin_b = jnp.tile(jnp.concatenate([zero, sin], axis=1), (1, LANES // HEAD_DIM))
    return cos_t, sin_a, sin_b


def _selection_constants(seq):
    n_cmp = (seq - CMP_BLOCK) // CMP_STRIDE + 1
    n_slc = seq // SLC_BLOCK
    ci = np.arange(N_CMP_PAD)[:, None]
    sj = np.arange(N_SLC_PAD)[None, :]
    ov = ((ci * CMP_STRIDE < (sj + 1) * SLC_BLOCK) & (ci * CMP_STRIDE + CMP_BLOCK > sj * SLC_BLOCK)
          & (ci < n_cmp) & (sj < n_slc))
    ovt = ov.T[:n_slc]
    ebt = (np.arange(seq)[:, None] // SLC_BLOCK) == np.arange(N_SLC_PAD)[None, :]
    ebt = ebt.reshape(seq // TQ, TQ, N_SLC_PAD)
    return jnp.asarray(ovt, BF16), jnp.asarray(ebt, BF16)


def _chunk_cols(w, n_chunks):
    return jnp.moveaxis(w.reshape(w.shape[:-1] + (n_chunks, FF_CHUNK)), -2, 0)


def _layer(x, attn_norm_w, w_in, pool_w, pool_scale, q_norm_w, k_norm_w, cmp_pos, cmp_w1, cmp_b1, cmp_w2,
           w_pool_br, w_attn_br, w_o, ffn_norm_w, w_up, conv_w, conv_b, w_down):
    batch, seq, _ = x.shape
    t = batch * seq
    x2 = x.reshape(t, D_MODEL)

    w_perm = jnp.concatenate([w_in[:, a:b] for a, b in _w_in_columns()], axis=1)
    w_perm = jnp.pad(w_perm, ((0, 0), (0, _N_SLABS * LANES - w_perm.shape[1]))).astype(BF16)
    normw = jnp.concatenate([jnp.tile(q_norm_w[None, :], (1, 2)), jnp.tile(k_norm_w, (1, 2))], axis=0)
    cos_t, sin_a, sin_b = _rope_tables(seq)
    u_pool, q, kvc, ks, kw, vs, vw, mg, gate = _inproj(
        x2, attn_norm_w[None, :], w_perm, normw, cos_t, sin_a, sin_b, seq)

    w1 = cmp_w1.reshape(2, CMP_BLOCK, HEAD_DIM, CMP_HIDDEN)
    zero = jnp.zeros_like(w1)
    w1bd = jnp.concatenate([jnp.concatenate([w1, zero], axis=-1), jnp.concatenate([zero, w1], axis=-1)],
                           axis=2).astype(BF16)
    pos128 = jnp.tile(cmp_pos, (1, 1, N_KV_GROUPS))
    b1 = jnp.tile(cmp_b1, (1, N_KV_GROUPS))[:, None, :]
    w2d = jnp.concatenate([cmp_w2, cmp_w2], axis=-1).astype(BF16)
    cmp_kv = _compress(kvc, pos128, w1bd, b1, w2d, batch, seq)

    ovt_mat, ebt_mat = _selection_constants(seq)
    attn = _attention_t(q, ks, kw, vs, vw, cmp_kv, gate, ovt_mat, ebt_mat, batch, seq)

    x1 = _merge(x2, u_pool, attn, mg, pool_w.astype(BF16), pool_scale[None, :], w_pool_br.astype(BF16),
                w_attn_br.astype(BF16), w_o.astype(BF16), seq)

    pad = D_FF_PAD - D_FF

    def pad_halves(w):
        lead = [(0, 0)] * (w.ndim - 1)
        return jnp.concatenate([jnp.pad(w[..., :D_FF], lead + [(0, pad)]),
                                jnp.pad(w[..., D_FF:], lead + [(0, pad)])], axis=-1)

    wup = _chunk_cols(pad_halves(w_up).astype(BF16), 2 * N_FF_CHUNKS)
    cw = _chunk_cols(jnp.pad(pad_halves(conv_w), ((0, CONV_HALO - CONV_WIDTH), (0, 0))), 2 * N_FF_CHUNKS)
    cb = _chunk_cols(pad_halves(conv_b)[None, :], 2 * N_FF_CHUNKS)
    wdn = jnp.pad(w_down, ((0, pad), (0, 0))).astype(BF16)
    out = _ffn(x1, ffn_norm_w[None, :], wup, cw, cb, wdn, seq)
    return out.reshape(batch, seq, D_MODEL)


def kernel(x, attn_norm_w, w_in, pool_w, pool_scale, q_norm_w, k_norm_w, cmp_pos, cmp_w1, cmp_b1, cmp_w2,
           w_pool_br, w_attn_br, w_o, ffn_norm_w, w_up, conv_w, conv_b, w_down):
    for l in range(attn_norm_w.shape[0]):
        x = _layer(x, attn_norm_w[l], w_in[l], pool_w[l], pool_scale[l], q_norm_w[l], k_norm_w[l],
                   cmp_pos[l], cmp_w1[l], cmp_b1[l], cmp_w2[l], w_pool_br[l], w_attn_br[l], w_o[l],
                   ffn_norm_w[l], w_up[l], conv_w[l], conv_b[l], w_down[l])
    return x
```

```python
import functools

import numpy as np
import jax
import jax.numpy as jnp
from jax import lax
from jax.experimental import pallas as pl
from jax.experimental.pallas import tpu as pltpu

D_MODEL = 1024
POOL_WINDOWS = (2, 4, 8, 16)
N_POOL_GROUPS = 4
POOL_GROUP = 128
POOL_WIDTH = 512
HEAD_DIM = 64
N_HEADS = 16
N_KV_GROUPS = 2
HEADS_PER_GROUP = 8
ATTN_WIDTH = 1024
N_BRANCH = 3
KV_WIDTH = 768
CMP_BLOCK = 32
CMP_STRIDE = 16
CMP_HIDDEN = 128
SLC_BLOCK = 64
SLC_TOP_N = 8
WINDOW = 512
ROPE_THETA = 10000.0
SCALE = HEAD_DIM ** -0.5
FORCE_BONUS = 1000.0
NEG_INF = -1e30
OFF_Q = POOL_WIDTH
OFF_KV = OFF_Q + ATTN_WIDTH
OFF_NSA_G = OFF_KV + KV_WIDTH
OFF_MERGE_G = OFF_NSA_G + N_BRANCH * N_HEADS
D_FF = 2752
CONV_WIDTH = 3
EPS = 1e-6
LOG2E = 1.4426950408889634

LANES = 128
MXU_N = 256
VMEM_LIMIT = 56 * 1024 * 1024

TM = 512
TQ = 256
SCORE_LOOKAHEAD = 5
VT_ROWS = 80
D_FF_PAD = 2816
FF_CHUNK = 256
N_FF_CHUNKS = D_FF_PAD // FF_CHUNK
POOL_HALO = 16
CONV_HALO = 8
N_CMP_PAD = 128
N_SLC_PAD = 128

F32 = jnp.float32
BF16 = jnp.bfloat16


def _dot(a, b):
    return jnp.dot(a, b, preferred_element_type=F32)


def _dot_nt(a, b):
    return lax.dot_general(a, b, (((1,), (1,)), ((), ())), preferred_element_type=F32)


_SLAB_POOL = 0
_SLAB_Q = 4
_SLAB_K = 12
_SLAB_V = 15
_SLAB_MERGE = 18
_SLAB_GATE = 34
_N_SLABS = 35


def _inproj_kernel(x_ref, anw_ref, w_ref, nw_ref, cos_ref, sa_ref, sb_ref,
                   up_ref, q_ref, kvc_ref, ks_ref, kw_ref, vs_ref, vw_ref, mg_ref, g_ref):
    x = x_ref[...]
    ms = jnp.mean(x * x, axis=-1, keepdims=True)
    h = ((x * lax.rsqrt(ms + EPS)) * anw_ref[...]).astype(BF16)
    tm = x.shape[0]
    lane = lax.broadcasted_iota(jnp.int32, (tm, LANES), 1)
    lo = lane < HEAD_DIM
    cos = cos_ref[...]
    sa = sa_ref[...]
    sb = sb_ref[...]

    def norm_rope(z, nw, scale):
        ss = z * z
        s_lo = jnp.sum(jnp.where(lo, ss, 0.0), axis=-1, keepdims=True)
        s_hi = jnp.sum(jnp.where(lo, 0.0, ss), axis=-1, keepdims=True)
        msq = jnp.where(lo, s_lo, s_hi) * (1.0 / HEAD_DIM)
        y = (z * lax.rsqrt(msq + EPS)) * nw
        r = y * cos + pltpu.roll(y, LANES - HEAD_DIM // 2, 1) * sa + pltpu.roll(y, HEAD_DIM // 2, 1) * sb
        return r * scale if scale is not None else r

    k_refs = (None, ks_ref, kw_ref)
    v_refs = (None, vs_ref, vw_ref)

    def emit(slab, z):
        if slab < _SLAB_Q:
            c = (slab - _SLAB_POOL) * LANES
            up_ref[:, c:c + LANES] = z
        elif slab < _SLAB_K:
            c = (slab - _SLAB_Q) * LANES
            q_ref[:, c:c + LANES] = norm_rope(z, nw_ref[0:1, :], SCALE * LOG2E).astype(BF16)
        elif slab < _SLAB_V:
            br = slab - _SLAB_K
            r = norm_rope(z, nw_ref[1 + br:2 + br, :], None)
            if br == 0:
                kvc_ref[:, 0:LANES] = r
            else:
                k_refs[br][...] = r.astype(BF16)
        elif slab < _SLAB_MERGE:
            br = slab - _SLAB_V
            if br == 0:
                kvc_ref[:, LANES:2 * LANES] = z
            else:
                v_refs[br][...] = z.astype(BF16)
        elif slab < _SLAB_GATE:
            c = (slab - _SLAB_MERGE) * LANES
            mg_ref[:, c:c + LANES] = jax.nn.sigmoid(z).astype(BF16)
        else:
            g_ref[...] = jax.nn.sigmoid(z)

    slab = 0
    while slab < _N_SLABS:
        width = 2 if slab + 1 < _N_SLABS else 1
        z = _dot(h, w_ref[:, slab * LANES:(slab + width) * LANES])
        for j in range(width):
            emit(slab + j, z[:, j * LANES:(j + 1) * LANES])
        slab += width


def _inproj(x2, anw, w_perm, normw, cos_t, sin_a, sin_b, seq):
    t = x2.shape[0]
    tiles_per_seq = seq // TM
    row = lambda i: (i, 0)
    const = lambda i: (0, 0)
    pos = lambda i: (i % tiles_per_seq, 0)
    out_shapes = (
        jax.ShapeDtypeStruct((t, POOL_WIDTH), F32),
        jax.ShapeDtypeStruct((t, ATTN_WIDTH), BF16),
        jax.ShapeDtypeStruct((t, 2 * LANES), F32),
        jax.ShapeDtypeStruct((t, LANES), BF16),
        jax.ShapeDtypeStruct((t, LANES), BF16),
        jax.ShapeDtypeStruct((t, LANES), BF16),
        jax.ShapeDtypeStruct((t, LANES), BF16),
        jax.ShapeDtypeStruct((t, 2 * D_MODEL), BF16),
        jax.ShapeDtypeStruct((t, LANES), F32),
    )
    out_specs = tuple(pl.BlockSpec((TM, s.shape[1]), row) for s in out_shapes)
    return pl.pallas_call(
        _inproj_kernel,
        grid=(t // TM,),
        in_specs=[
            pl.BlockSpec((TM, D_MODEL), row),
            pl.BlockSpec((1, D_MODEL), const),
            pl.BlockSpec((D_MODEL, _N_SLABS * LANES), const),
            pl.BlockSpec((4, LANES), const),
            pl.BlockSpec((TM, LANES), pos),
            pl.BlockSpec((TM, LANES), pos),
            pl.BlockSpec((TM, LANES), pos),
        ],
        out_specs=out_specs,
        out_shape=out_shapes,
        compiler_params=pltpu.CompilerParams(
            dimension_semantics=("arbitrary",), vmem_limit_bytes=VMEM_LIMIT),
        name="inproj",
    )(x2, anw, w_perm, normw, cos_t, sin_a, sin_b)


def _compress_kernel(x_ref, pos_ref, w1_ref, b1_ref, w2_ref, w2t_ref, o_ref):
    n = x_ref.shape[0] // CMP_STRIDE
    top = None
    bot = None
    for l in range(CMP_STRIDE):
        y = x_ref[pl.ds(l, n, stride=CMP_STRIDE), :]
        t = _dot((y + pos_ref[0, l:l + 1, :]).astype(BF16), w1_ref[0, l])
        b = _dot((y + pos_ref[0, CMP_STRIDE + l:CMP_STRIDE + l + 1, :]).astype(BF16), w1_ref[0, CMP_STRIDE + l])
        top = t if top is None else top + t
        bot = b if bot is None else bot + b
    hid = top + pltpu.roll(bot, n - 1, 0) + b1_ref[0]
    act = jax.nn.gelu(hid).astype(BF16)
    @pl.when(pl.program_id(0) == 0)
    def _keys():
        for g in range(N_KV_GROUPS):
            o_ref[0, 0, g] = _dot(act[:, g * CMP_HIDDEN:(g + 1) * CMP_HIDDEN], w2_ref[0]).astype(o_ref.dtype)

    @pl.when(pl.program_id(0) == 1)
    def _values():
        for g in range(N_KV_GROUPS):
            o_ref[0, 0, g] = _dot_nt(w2t_ref[0], act[:, g * CMP_HIDDEN:(g + 1) * CMP_HIDDEN]).astype(o_ref.dtype)


def _compress(kvc, pos128, w1bd, b1, w2d, batch, seq):
    n = seq // CMP_STRIDE
    assert n == N_CMP_PAD == LANES
    w2t = jnp.swapaxes(w2d, 1, 2)
    return pl.pallas_call(
        _compress_kernel,
        grid=(2, batch),
        in_specs=[
            pl.BlockSpec((seq, LANES), lambda a, i: (i, a)),
            pl.BlockSpec((1, CMP_BLOCK, LANES), lambda a, i: (a, 0, 0)),
            pl.BlockSpec((1, CMP_BLOCK, LANES, N_KV_GROUPS * CMP_HIDDEN), lambda a, i: (a, 0, 0, 0)),
            pl.BlockSpec((1, 1, N_KV_GROUPS * CMP_HIDDEN), lambda a, i: (a, 0, 0)),
            pl.BlockSpec((1, CMP_HIDDEN, LANES), lambda a, i: (a, 0, 0)),
            pl.BlockSpec((1, LANES, CMP_HIDDEN), lambda a, i: (a, 0, 0)),
        ],
        out_specs=pl.BlockSpec((1, 1, N_KV_GROUPS, n, LANES), lambda a, i: (a, i, 0, 0, 0)),
        out_shape=jax.ShapeDtypeStruct((2, batch, N_KV_GROUPS, n, LANES), BF16),
        compiler_params=pltpu.CompilerParams(
            dimension_semantics=("arbitrary", "arbitrary"), vmem_limit_bytes=VMEM_LIMIT),
        name="compress",
    )(kvc, pos128, w1bd, b1, w2d, w2t)


def _attn_t_kernel(q_ref, ks_ref, kw_ref, vs_ref, vw_ref, ck_ref, cvt_ref, gate_ref, ovt_ref, ebt_ref,
                   o_ref, kks, kkw, vts, vtw, qst, sel_ref, oc_ref, m_s, m_w, acc_s, acc_w):
    g = pl.program_id(1)
    i = pl.program_id(2)
    seq = ks_ref.shape[0]
    nk = seq // TQ
    n_pairs = HEADS_PER_GROUP // 2

    @pl.when(i == 0)
    def _prep():
        lane = lax.broadcasted_iota(jnp.int32, (TQ, LANES), 1)
        mine = (lane >= HEAD_DIM).astype(jnp.int32) == g
        ones = jnp.ones((VT_ROWS - HEAD_DIM, TQ), F32)

        def both_halves(x):
            xr = jnp.concatenate([x[:, HEAD_DIM:], x[:, :HEAD_DIM]], axis=1)
            return jnp.where(mine, x, xr)

        for kt in range(nk):
            keys = slice(kt * TQ, (kt + 1) * TQ)
            kks[keys, :] = both_halves(ks_ref[keys, :])
            kkw[keys, :] = both_halves(kw_ref[keys, :])
            for src, dst in ((vs_ref, vts), (vw_ref, vtw)):
                vt = both_halves(src[keys, :]).astype(F32).T
                dst[kt] = jnp.concatenate([vt[:HEAD_DIM, :], ones], axis=0).astype(BF16)

    lane_q = lax.broadcasted_iota(jnp.int32, (TQ, LANES), 1)
    lo_q = lane_q < HEAD_DIM
    for p in range(n_pairs):
        qp = q_ref[:, p * LANES:(p + 1) * LANES]
        zero = jnp.zeros_like(qp)
        qst[p * TQ:(p + 1) * TQ, :] = jnp.where(lo_q, qp, zero)
        qst[(n_pairs + p) * TQ:(n_pairs + p + 1) * TQ, :] = jnp.where(lo_q, zero, qp)

    def q_block(hb):
        return qst[hb * TQ:(hb + 1) * TQ, :]

    cmp_tok = lax.broadcasted_iota(jnp.int32, (N_CMP_PAD, TQ), 0)
    tq_c = i * TQ + lax.broadcasted_iota(jnp.int32, (N_CMP_PAD, TQ), 1)
    cmp_mask = (cmp_tok * CMP_STRIDE + (CMP_BLOCK - 1)) <= tq_c
    ck = ck_ref[0, 0, 0]
    cvt = cvt_ref[0, 0, 0]
    def cmp_head(hb):
        s = jnp.where(cmp_mask, _dot_nt(ck, q_block(hb)), NEG_INF)
        m = jnp.max(s, axis=0, keepdims=True)
        e = jnp.where(cmp_mask, jnp.exp2(s - m), 0.0)
        l = jnp.sum(e, axis=0, keepdims=True)
        pc = e * jnp.where(l > 0.0, 1.0 / l, 0.0)
        oc_ref[hb] = _dot(cvt, pc.astype(BF16))[:HEAD_DIM, :]
        return pc

    def select_blocks():
        pcs = functools.reduce(lambda a, b: a + b, [cmp_head(hb) for hb in range(HEADS_PER_GROUP)])
        pcs_hi = pcs.astype(BF16)
        pcs_lo = (pcs - pcs_hi.astype(F32)).astype(BF16)
        imp = _dot(ovt_ref[...], pcs_hi) + _dot(ovt_ref[...], pcs_lo)
        n_blk = seq // SLC_BLOCK
        blk = lax.broadcasted_iota(jnp.int32, (n_blk, TQ), 0)
        tq_t = i * TQ + lax.broadcasted_iota(jnp.int32, (n_blk, TQ), 1)
        cur = lax.shift_right_logical(tq_t, SLC_BLOCK.bit_length() - 1)
        forced = (blk == 0) | (blk == cur) | (blk == cur - 1)
        valid = blk * SLC_BLOCK <= tq_t
        score = jnp.where(valid, imp + jnp.where(forced, FORCE_BONUS, 0.0), NEG_INF)
        blk_f = blk.astype(F32)
        sel_t = jnp.zeros((n_blk, TQ), F32)
        for _ in range(SLC_TOP_N):
            mx = jnp.max(score, axis=0, keepdims=True)
            first = jnp.min(jnp.where(score == mx, blk_f, float(n_blk)), axis=0, keepdims=True)
            pick = blk_f == first
            sel_t = jnp.where(pick, 1.0, sel_t)
            score = jnp.where(pick, -jnp.inf, score)
        sel_ref[...] = jnp.concatenate([sel_t, jnp.zeros((N_SLC_PAD - n_blk, TQ), F32)], axis=0).astype(BF16)

    key_row = lax.broadcasted_iota(jnp.int32, (TQ, TQ), 0)
    q_col = lax.broadcasted_iota(jnp.int32, (TQ, TQ), 1)
    causal_bias = jnp.where(key_row <= q_col, 0.0, NEG_INF)
    tail_bias = jnp.where(key_row > q_col, 0.0, NEG_INF)

    def run_steps(steps):
        def score_stage(ctx, hb):
            k_tile, _, _, m_ref, bias, first = ctx
            s = _dot_nt(k_tile, q_block(hb))
            if bias is not None:
                s = s + bias
            m_new = jnp.max(s, axis=0, keepdims=True)
            if first:
                alpha = None
            else:
                m_old = m_ref[hb]
                m_new = jnp.maximum(m_old, m_new)
                alpha = jnp.exp2(m_old - m_new)
            m_ref[hb] = m_new
            return jnp.exp2(s - m_new).astype(BF16), alpha

        def value_stage(ctx, hb, p, alpha):
            _, vt_ref, acc_ref, _, _, _ = ctx
            pv = _dot(vt_ref[0][vt_ref[1]], p)
            acc_ref[hb] = pv if alpha is None else alpha * acc_ref[hb] + pv

        pending = []
        for k_ref, vt_ref, acc_ref, m_ref, kt, bias_fn, first in steps:
            keys = pl.ds(pl.multiple_of(kt * TQ, TQ), TQ)
            ctx = (k_ref[keys, :], (vt_ref, kt), acc_ref, m_ref, bias_fn(), first)
            for hb in range(HEADS_PER_GROUP):
                pending.append((ctx, hb) + score_stage(ctx, hb))
                if len(pending) > SCORE_LOOKAHEAD:
                    value_stage(*pending.pop(0))
        for item in pending:
            value_stage(*item)

    def sel_bias(kt, extra=None):
        b = (_dot(ebt_ref[kt], sel_ref[...]) - 1.0) * (-NEG_INF)
        return b if extra is None else b + extra

    def win(kt, bias, first=False):
        return (kkw, vtw, acc_w, m_w, kt, lambda: bias, first)

    def sel(kt):
        return (kks, vts, acc_s, m_s, kt, lambda: sel_bias(kt), False)

    def first_block(n_win):
        def first_sel_bias():
            select_blocks()
            return sel_bias(i, causal_bias)

        steps = [win(i, causal_bias, True)]
        if n_win >= 2:
            steps.append(win(i - 1, None))
        if n_win >= 3:
            steps.append(win(i - 2, tail_bias))
        steps.append((kks, vts, acc_s, m_s, i, first_sel_bias, True))
        run_steps(steps)

    pl.when(i == 0)(functools.partial(first_block, 1))
    pl.when(i == 1)(functools.partial(first_block, 2))
    pl.when(i >= 2)(functools.partial(first_block, 3))

    def sel_pair(j, carry):
        run_steps([sel(2 * j), sel(2 * j + 1)])
        return carry

    lax.fori_loop(0, lax.shift_right_logical(i, 1), sel_pair, 0)

    @pl.when((i & 1) == 1)
    def _sel_odd():
        run_steps([sel(i - 1)])

    gate_t = gate_ref[...].T

    def gate_row(head, br):
        r0 = head * N_BRANCH + br
        r1 = (HEADS_PER_GROUP + head) * N_BRANCH + br
        return jnp.where(g == 0, gate_t[r0:r0 + 1, :], gate_t[r1:r1 + 1, :])

    outs = []
    for head in range(HEADS_PER_GROUP):
        hb = (head % 2) * n_pairs + head // 2
        t = gate_row(head, 0) * oc_ref[hb]
        for br, acc_ref in ((1, acc_s), (2, acc_w)):
            acc = acc_ref[hb]
            t = t + (gate_row(head, br) * (1.0 / acc[HEAD_DIM:HEAD_DIM + 1, :])) * acc[:HEAD_DIM, :]
        outs.append(t)
    o_ref[...] = jnp.concatenate(outs, axis=0).T.astype(o_ref.dtype)


def _attention_t(q, ks, kw, vs, vw, cmp_kv, gate, ovt_mat, ebt_mat, batch, seq):
    nq = seq // TQ
    gw = HEADS_PER_GROUP * HEAD_DIM
    kv_spec = pl.BlockSpec((seq, LANES), lambda b, g, i: (b, 0))
    return pl.pallas_call(
        _attn_t_kernel,
        grid=(batch, N_KV_GROUPS, nq),
        in_specs=[
            pl.BlockSpec((TQ, gw), lambda b, g, i: (b * nq + i, g)),
            kv_spec, kv_spec, kv_spec, kv_spec,
            pl.BlockSpec((1, 1, 1, N_CMP_PAD, LANES), lambda b, g, i: (0, b, g, 0, 0)),
            pl.BlockSpec((1, 1, 1, LANES, N_CMP_PAD), lambda b, g, i: (1, b, g, 0, 0)),
            pl.BlockSpec((TQ, LANES), lambda b, g, i: (b * nq + i, 0)),
            pl.BlockSpec((seq // SLC_BLOCK, N_CMP_PAD), lambda b, g, i: (0, 0)),
            pl.BlockSpec((nq, TQ, N_SLC_PAD), lambda b, g, i: (0, 0, 0)),
        ],
        out_specs=pl.BlockSpec((TQ, gw), lambda b, g, i: (b * nq + i, g)),
        out_shape=jax.ShapeDtypeStruct((batch * seq, ATTN_WIDTH), BF16),
        scratch_shapes=[
            pltpu.VMEM((seq, LANES), BF16), pltpu.VMEM((seq, LANES), BF16),
            pltpu.VMEM((nq, VT_ROWS, TQ), BF16), pltpu.VMEM((nq, VT_ROWS, TQ), BF16),
            pltpu.VMEM((HEADS_PER_GROUP * TQ, LANES), BF16),
            pltpu.VMEM((N_SLC_PAD, TQ), BF16),
            pltpu.VMEM((HEADS_PER_GROUP, HEAD_DIM, TQ), F32),
            pltpu.VMEM((HEADS_PER_GROUP, 1, TQ), F32),
            pltpu.VMEM((HEADS_PER_GROUP, 1, TQ), F32),
            pltpu.VMEM((HEADS_PER_GROUP, VT_ROWS, TQ), F32),
            pltpu.VMEM((HEADS_PER_GROUP, VT_ROWS, TQ), F32),
        ],
        compiler_params=pltpu.CompilerParams(
            dimension_semantics=("arbitrary", "arbitrary", "arbitrary"), vmem_limit_bytes=VMEM_LIMIT),
        name="nsa_attention",
    )(q, ks, kw, vs, vw, cmp_kv, cmp_kv, gate, ovt_mat, ebt_mat)


def _merge_kernel(x_ref, up_ref, at_ref, mg_ref, pw_ref, ps_ref, wpb_ref, wab_ref, wo_ref,
                  o_ref, halo_ref, *, tiles_per_seq):
    i = pl.program_id(0)
    tm = x_ref.shape[0]
    u = up_ref[...]

    @pl.when((i % tiles_per_seq) == 0)
    def _seq_start():
        halo_ref[...] = jnp.zeros_like(halo_ref)

    halo = halo_ref[...]
    halo_ref[...] = u[tm - POOL_HALO:, :]
    ue = jnp.concatenate([halo, u], axis=0)
    rowp = lax.broadcasted_iota(jnp.int32, (tm, POOL_GROUP), 0) + (i % tiles_per_seq) * tm
    ys = []
    for gi, w in enumerate(POOL_WINDOWS):
        xg = ue[:, gi * POOL_GROUP:(gi + 1) * POOL_GROUP]
        sw = xg
        k = 1
        while k < w:
            sw = sw + pltpu.roll(sw, k, 0)
            k *= 2
        cnt = jnp.minimum(rowp + 1, w).astype(F32)
        ug = u[:, gi * POOL_GROUP:(gi + 1) * POOL_GROUP]
        pooled = sw[POOL_HALO:, :] / cnt - ug
        ys.append(_dot(pooled.astype(BF16), pw_ref[gi]))
    y = jnp.concatenate(ys, axis=1) * ps_ref[...]
    y_pool = _dot(y.astype(BF16), wpb_ref[...])
    y_attn = _dot(at_ref[...], wab_ref[...])
    mg = mg_ref[...]
    merged = mg[:, :D_MODEL] * y_pool + mg[:, D_MODEL:] * y_attn
    o_ref[...] = x_ref[...] + _dot(merged.astype(BF16), wo_ref[...])


def _merge(x2, u_pool, attn, mg, pool_w, pool_scale, w_pool_br, w_attn_br, w_o, seq):
    t = x2.shape[0]
    row = lambda i: (i, 0)
    const = lambda i: (0, 0)
    return pl.pallas_call(
        functools.partial(_merge_kernel, tiles_per_seq=seq // TM),
        grid=(t // TM,),
        in_specs=[
            pl.BlockSpec((TM, D_MODEL), row),
            pl.BlockSpec((TM, POOL_WIDTH), row),
            pl.BlockSpec((TM, ATTN_WIDTH), row),
            pl.BlockSpec((TM, 2 * D_MODEL), row),
            pl.BlockSpec((N_POOL_GROUPS, POOL_GROUP, POOL_GROUP), lambda i: (0, 0, 0)),
            pl.BlockSpec((1, POOL_WIDTH), const),
            pl.BlockSpec((POOL_WIDTH, D_MODEL), const),
            pl.BlockSpec((ATTN_WIDTH, D_MODEL), const),
            pl.BlockSpec((D_MODEL, D_MODEL), const),
        ],
        out_specs=pl.BlockSpec((TM, D_MODEL), row),
        out_shape=jax.ShapeDtypeStruct((t, D_MODEL), F32),
        scratch_shapes=[pltpu.VMEM((POOL_HALO, POOL_WIDTH), F32)],
        compiler_params=pltpu.CompilerParams(
            dimension_semantics=("arbitrary",), vmem_limit_bytes=VMEM_LIMIT),
        name="pool_merge",
    )(x2, u_pool, attn, mg, pool_w, pool_scale, w_pool_br, w_attn_br, w_o)


def _ffn_kernel(x_ref, nw_ref, wup_ref, cw_ref, cb_ref, wdn_ref, o_ref, halo_ref, act_ref, *, tiles_per_seq):
    i = pl.program_id(0)
    tm = x_ref.shape[0]
    x = x_ref[...]
    ms = jnp.mean(x * x, axis=-1, keepdims=True)
    h = ((x * lax.rsqrt(ms + EPS)) * nw_ref[...]).astype(BF16)

    @pl.when((i % tiles_per_seq) == 0)
    def _seq_start():
        halo_ref[...] = jnp.zeros_like(halo_ref)

    def conv(c):
        u = _dot(h, wup_ref[c])
        halo = halo_ref[c]
        halo_ref[c] = u[tm - CONV_HALO:, :]
        ue = jnp.concatenate([halo, u], axis=0)
        w = cw_ref[c]
        u1 = pltpu.roll(ue, 1, 0)[CONV_HALO:, :]
        u2 = pltpu.roll(ue, 2, 0)[CONV_HALO:, :]
        return cb_ref[c] + w[0:1, :] * u2 + w[1:2, :] * u1 + w[2:3, :] * u

    for c in range(N_FF_CHUNKS):
        gate = conv(c)
        val = conv(N_FF_CHUNKS + c)
        act = (gate * jax.nn.sigmoid(gate)) * val
        act_ref[:, c * FF_CHUNK:(c + 1) * FF_CHUNK] = act.astype(BF16)
    o_ref[...] = x + _dot(act_ref[...], wdn_ref[...])


def _ffn(x1, nw, wup, cw, cb, wdn, seq):
    t = x1.shape[0]
    row = lambda i: (i, 0)
    c3 = lambda i: (0, 0, 0)
    return pl.pallas_call(
        functools.partial(_ffn_kernel, tiles_per_seq=seq // TM),
        grid=(t // TM,),
        in_specs=[
            pl.BlockSpec((TM, D_MODEL), row),
            pl.BlockSpec((1, D_MODEL), lambda i: (0, 0)),
            pl.BlockSpec((2 * N_FF_CHUNKS, D_MODEL, FF_CHUNK), c3),
            pl.BlockSpec((2 * N_FF_CHUNKS, CONV_HALO, FF_CHUNK), c3),
            pl.BlockSpec((2 * N_FF_CHUNKS, 1, FF_CHUNK), c3),
            pl.BlockSpec((D_FF_PAD, D_MODEL), lambda i: (0, 0)),
        ],
        out_specs=pl.BlockSpec((TM, D_MODEL), row),
        out_shape=jax.ShapeDtypeStruct((t, D_MODEL), F32),
        scratch_shapes=[pltpu.VMEM((2 * N_FF_CHUNKS, CONV_HALO, FF_CHUNK), F32),
                        pltpu.VMEM((TM, D_FF_PAD), BF16)],
        compiler_params=pltpu.CompilerParams(
            dimension_semantics=("arbitrary",), vmem_limit_bytes=VMEM_LIMIT),
        name="conv_ffn",
    )(x1, nw, wup, cw, cb, wdn)


def _w_in_columns():
    ranges = [(0, OFF_KV)]
    for kv in range(2):
        for br in range(N_BRANCH):
            base = OFF_KV + br * 4 * HEAD_DIM + kv * 2 * HEAD_DIM
            ranges.append((base, base + 2 * HEAD_DIM))
    ranges.append((OFF_MERGE_G, OFF_MERGE_G + 2 * D_MODEL))
    ranges.append((OFF_NSA_G, OFF_MERGE_G))
    return ranges


def _rope_tables(seq):
    half = HEAD_DIM // 2
    freqs = ROPE_THETA ** (-jnp.arange(half, dtype=F32) / half)
    ang = jnp.arange(seq, dtype=F32)[:, None] * freqs[None, :]
    cos = jnp.cos(ang)
    sin = jnp.sin(ang)
    zero = jnp.zeros_like(sin)
    cos_t = jnp.tile(cos, (1, LANES // half))
    sin_a = jnp.tile(jnp.concatenate([-sin, zero], axis=1), (1, LANES // HEAD_DIM))
    sin_b = jnp.tile(jnp.concatenate([zero, sin], axis=1), (1, LANES // HEAD_DIM))
    return cos_t, sin_a, sin_b


def _selection_constants(seq):
    n_cmp = (seq - CMP_BLOCK) // CMP_STRIDE + 1
    n_slc = seq // SLC_BLOCK
    ci = np.arange(N_CMP_PAD)[:, None]
    sj = np.arange(N_SLC_PAD)[None, :]
    ov = ((ci * CMP_STRIDE < (sj + 1) * SLC_BLOCK) & (ci * CMP_STRIDE + CMP_BLOCK > sj * SLC_BLOCK)
          & (ci < n_cmp) & (sj < n_slc))
    ovt = ov.T[:n_slc]
    ebt = (np.arange(seq)[:, None] // SLC_BLOCK) == np.arange(N_SLC_PAD)[None, :]
    ebt = ebt.reshape(seq // TQ, TQ, N_SLC_PAD)
    return jnp.asarray(ovt, BF16), jnp.asarray(ebt, BF16)


def _chunk_cols(w, n_chunks):
    return jnp.moveaxis(w.reshape(w.shape[:-1] + (n_chunks, FF_CHUNK)), -2, 0)


def _layer(x, attn_norm_w, w_in, pool_w, pool_scale, q_norm_w, k_norm_w, cmp_pos, cmp_w1, cmp_b1, cmp_w2,
           w_pool_br, w_attn_br, w_o, ffn_norm_w, w_up, conv_w, conv_b, w_down):
    batch, seq, _ = x.shape
    t = batch * seq
    x2 = x.reshape(t, D_MODEL)

    w_perm = jnp.concatenate([w_in[:, a:b] for a, b in _w_in_columns()], axis=1)
    w_perm = jnp.pad(w_perm, ((0, 0), (0, _N_SLABS * LANES - w_perm.shape[1]))).astype(BF16)
    normw = jnp.concatenate([jnp.tile(q_norm_w[None, :], (1, 2)), jnp.tile(k_norm_w, (1, 2))], axis=0)
    cos_t, sin_a, sin_b = _rope_tables(seq)
    u_pool, q, kvc, ks, kw, vs, vw, mg, gate = _inproj(
        x2, attn_norm_w[None, :], w_perm, normw, cos_t, sin_a, sin_b, seq)

    w1 = cmp_w1.reshape(2, CMP_BLOCK, HEAD_DIM, CMP_HIDDEN)
    zero = jnp.zeros_like(w1)
    w1bd = jnp.concatenate([jnp.concatenate([w1, zero], axis=-1), jnp.concatenate([zero, w1], axis=-1)],
                           axis=2).astype(BF16)
    pos128 = jnp.tile(cmp_pos, (1, 1, N_KV_GROUPS))
    b1 = jnp.tile(cmp_b1, (1, N_KV_GROUPS))[:, None, :]
    w2d = jnp.concatenate([cmp_w2, cmp_w2], axis=-1).astype(BF16)
    cmp_kv = _compress(kvc, pos128, w1bd, b1, w2d, batch, seq)

    ovt_mat, ebt_mat = _selection_constants(seq)
    attn = _attention_t(q, ks, kw, vs, vw, cmp_kv, gate, ovt_mat, ebt_mat, batch, seq)

    x1 = _merge(x2, u_pool, attn, mg, pool_w.astype(BF16), pool_scale[None, :], w_pool_br.astype(BF16),
                w_attn_br.astype(BF16), w_o.astype(BF16), seq)

    pad = D_FF_PAD - D_FF

    def pad_halves(w):
        lead = [(0, 0)] * (w.ndim - 1)
        return jnp.concatenate([jnp.pad(w[..., :D_FF], lead + [(0, pad)]),
                                jnp.pad(w[..., D_FF:], lead + [(0, pad)])], axis=-1)

    wup = _chunk_cols(pad_halves(w_up).astype(BF16), 2 * N_FF_CHUNKS)
    cw = _chunk_cols(jnp.pad(pad_halves(conv_w), ((0, CONV_HALO - CONV_WIDTH), (0, 0))), 2 * N_FF_CHUNKS)
    cb = _chunk_cols(pad_halves(conv_b)[None, :], 2 * N_FF_CHUNKS)
    wdn = jnp.pad(w_down, ((0, pad), (0, 0))).astype(BF16)
    out = _ffn(x1, ffn_norm_w[None, :], wup, cw, cb, wdn, seq)
    return out.reshape(batch, seq, D_MODEL)


def kernel(x, attn_norm_w, w_in, pool_w, pool_scale, q_norm_w, k_norm_w, cmp_pos, cmp_w1, cmp_b1, cmp_w2,
           w_pool_br, w_attn_br, w_o, ffn_norm_w, w_up, conv_w, conv_b, w_down):
    for l in range(attn_norm_w.shape[0]):
        x = _layer(x, attn_norm_w[l], w_in[l], pool_w[l], pool_scale[l], q_norm_w[l], k_norm_w[l],
                   cmp_pos[l], cmp_w1[l], cmp_b1[l], cmp_w2[l], w_pool_br[l], w_attn_br[l], w_o[l],
                   ffn_norm_w[l], w_up[l], conv_w[l], conv_b[l], w_down[l])
    return x
```

```python
import functools

import numpy as np
import jax
import jax.numpy as jnp
from jax import lax
from jax.experimental import pallas as pl
from jax.experimental.pallas import tpu as pltpu

D_MODEL = 1024
POOL_WINDOWS = (2, 4, 8, 16)
N_POOL_GROUPS = 4
POOL_GROUP = 128
POOL_WIDTH = 512
HEAD_DIM = 64
N_HEADS = 16
N_KV_GROUPS = 2
HEADS_PER_GROUP = 8
ATTN_WIDTH = 1024
N_BRANCH = 3
KV_WIDTH = 768
CMP_BLOCK = 32
CMP_STRIDE = 16
CMP_HIDDEN = 128
SLC_BLOCK = 64
SLC_TOP_N = 8
WINDOW = 512
ROPE_THETA = 10000.0
SCALE = HEAD_DIM ** -0.5
FORCE_BONUS = 1000.0
NEG_INF = -1e30
OFF_Q = POOL_WIDTH
OFF_KV = OFF_Q + ATTN_WIDTH
OFF_NSA_G = OFF_KV + KV_WIDTH
OFF_MERGE_G = OFF_NSA_G + N_BRANCH * N_HEADS
D_FF = 2752
CONV_WIDTH = 3
EPS = 1e-6
LOG2E = 1.4426950408889634

LANES = 128
MXU_N = 256
VMEM_LIMIT = 56 * 1024 * 1024

TM = 512
TQ = 256
SCORE_LOOKAHEAD = 5
VT_ROWS = 80
D_FF_PAD = 2816
FF_CHUNK = 256
N_FF_CHUNKS = D_FF_PAD // FF_CHUNK
POOL_HALO = 16
CONV_HALO = 8
N_CMP_PAD = 128
N_SLC_PAD = 128

F32 = jnp.float32
BF16 = jnp.bfloat16


def _dot(a, b):
    return jnp.dot(a, b, preferred_element_type=F32)


def _dot_nt(a, b):
    return lax.dot_general(a, b, (((1,), (1,)), ((), ())), preferred_element_type=F32)


_SLAB_POOL = 0
_SLAB_Q = 4
_SLAB_K = 12
_SLAB_V = 15
_SLAB_MERGE = 18
_SLAB_GATE = 34
_N_SLABS = 35


def _inproj_kernel(x_ref, anw_ref, w_ref, nw_ref, cos_ref, sa_ref, sb_ref,
                   up_ref, q_ref, kvc_ref, ks_ref, kw_ref, vs_ref, vw_ref, mg_ref, g_ref):
    x = x_ref[...]
    ms = jnp.mean(x * x, axis=-1, keepdims=True)
    h = ((x * lax.rsqrt(ms + EPS)) * anw_ref[...]).astype(BF16)
    tm = x.shape[0]
    lane = lax.broadcasted_iota(jnp.int32, (tm, LANES), 1)
    lo = lane < HEAD_DIM
    cos = cos_ref[...]
    sa = sa_ref[...]
    sb = sb_ref[...]

    def norm_rope(z, nw, scale):
        ss = z * z
        s_lo = jnp.sum(jnp.where(lo, ss, 0.0), axis=-1, keepdims=True)
        s_hi = jnp.sum(jnp.where(lo, 0.0, ss), axis=-1, keepdims=True)
        msq = jnp.where(lo, s_lo, s_hi) * (1.0 / HEAD_DIM)
        y = (z * lax.rsqrt(msq + EPS)) * nw
        r = y * cos + pltpu.roll(y, LANES - HEAD_DIM // 2, 1) * sa + pltpu.roll(y, HEAD_DIM // 2, 1) * sb
        return r * scale if scale is not None else r

    k_refs = (None, ks_ref, kw_ref)
    v_refs = (None, vs_ref, vw_ref)

    def emit(slab, z):
        if slab < _SLAB_Q:
            c = (slab - _SLAB_POOL) * LANES
            up_ref[:, c:c + LANES] = z
        elif slab < _SLAB_K:
            c = (slab - _SLAB_Q) * LANES
            q_ref[:, c:c + LANES] = norm_rope(z, nw_ref[0:1, :], SCALE * LOG2E).astype(BF16)
        elif slab < _SLAB_V:
            br = slab - _SLAB_K
            r = norm_rope(z, nw_ref[1 + br:2 + br, :], None)
            if br == 0:
                kvc_ref[:, 0:LANES] = r
            else:
                k_refs[br][...] = r.astype(BF16)
        elif slab < _SLAB_MERGE:
            br = slab - _SLAB_V
            if br == 0:
                kvc_ref[:, LANES:2 * LANES] = z
            else:
                v_refs[br][...] = z.astype(BF16)
        elif slab < _SLAB_GATE:
            c = (slab - _SLAB_MERGE) * LANES
            mg_ref[:, c:c + LANES] = jax.nn.sigmoid(z).astype(BF16)
        else:
            g_ref[...] = jax.nn.sigmoid(z)

    slab = 0
    while slab < _N_SLABS:
        width = 2 if slab + 1 < _N_SLABS else 1
        z = _dot(h, w_ref[:, slab * LANES:(slab + width) * LANES])
        for j in range(width):
            emit(slab + j, z[:, j * LANES:(j + 1) * LANES])
        slab += width


def _inproj(x2, anw, w_perm, normw, cos_t, sin_a, sin_b, seq):
    t = x2.shape[0]
    tiles_per_seq = seq // TM
    row = lambda i: (i, 0)
    const = lambda i: (0, 0)
    pos = lambda i: (i % tiles_per_seq, 0)
    out_shapes = (
        jax.ShapeDtypeStruct((t, POOL_WIDTH), F32),
        jax.ShapeDtypeStruct((t, ATTN_WIDTH), BF16),
        jax.ShapeDtypeStruct((t, 2 * LANES), F32),
        jax.ShapeDtypeStruct((t, LANES), BF16),
        jax.ShapeDtypeStruct((t, LANES), BF16),
        jax.ShapeDtypeStruct((t, LANES), BF16),
        jax.ShapeDtypeStruct((t, LANES), BF16),
        jax.ShapeDtypeStruct((t, 2 * D_MODEL), BF16),
        jax.ShapeDtypeStruct((t, LANES), F32),
    )
    out_specs = tuple(pl.BlockSpec((TM, s.shape[1]), row) for s in out_shapes)
    return pl.pallas_call(
        _inproj_kernel,
        grid=(t // TM,),
        in_specs=[
            pl.BlockSpec((TM, D_MODEL), row),
            pl.BlockSpec((1, D_MODEL), const),
            pl.BlockSpec((D_MODEL, _N_SLABS * LANES), const),
            pl.BlockSpec((4, LANES), const),
            pl.BlockSpec((TM, LANES), pos),
            pl.BlockSpec((TM, LANES), pos),
            pl.BlockSpec((TM, LANES), pos),
        ],
        out_specs=out_specs,
        out_shape=out_shapes,
        compiler_params=pltpu.CompilerParams(
            dimension_semantics=("arbitrary",), vmem_limit_bytes=VMEM_LIMIT),
        name="inproj",
    )(x2, anw, w_perm, normw, cos_t, sin_a, sin_b)


def _compress_kernel(x_ref, pos_ref, w1_ref, b1_ref, w2_ref, w2t_ref, o_ref):
    n = x_ref.shape[0] // CMP_STRIDE
    top = None
    bot = None
    for l in range(CMP_STRIDE):
        y = x_ref[pl.ds(l, n, stride=CMP_STRIDE), :]
        t = _dot((y + pos_ref[0, l:l + 1, :]).astype(BF16), w1_ref[0, l])
        b = _dot((y + pos_ref[0, CMP_STRIDE + l:CMP_STRIDE + l + 1, :]).astype(BF16), w1_ref[0, CMP_STRIDE + l])
        top = t if top is None else top + t
        bot = b if bot is None else bot + b
    hid = top + pltpu.roll(bot, n - 1, 0) + b1_ref[0]
    act = jax.nn.gelu(hid).astype(BF16)
    @pl.when(pl.program_id(0) == 0)
    def _keys():
        for g in range(N_KV_GROUPS):
            o_ref[0, 0, g] = _dot(act[:, g * CMP_HIDDEN:(g + 1) * CMP_HIDDEN], w2_ref[0]).astype(o_ref.dtype)

    @pl.when(pl.program_id(0) == 1)
    def _values():
        for g in range(N_KV_GROUPS):
            o_ref[0, 0, g] = _dot_nt(w2t_ref[0], act[:, g * CMP_HIDDEN:(g + 1) * CMP_HIDDEN]).astype(o_ref.dtype)


def _compress(kvc, pos128, w1bd, b1, w2d, batch, seq):
    n = seq // CMP_STRIDE
    assert n == N_CMP_PAD == LANES
    w2t = jnp.swapaxes(w2d, 1, 2)
    return pl.pallas_call(
        _compress_kernel,
        grid=(2, batch),
        in_specs=[
            pl.BlockSpec((seq, LANES), lambda a, i: (i, a)),
            pl.BlockSpec((1, CMP_BLOCK, LANES), lambda a, i: (a, 0, 0)),
            pl.BlockSpec((1, CMP_BLOCK, LANES, N_KV_GROUPS * CMP_HIDDEN), lambda a, i: (a, 0, 0, 0)),
            pl.BlockSpec((1, 1, N_KV_GROUPS * CMP_HIDDEN), lambda a, i: (a, 0, 0)),
            pl.BlockSpec((1, CMP_HIDDEN, LANES), lambda a, i: (a, 0, 0)),
            pl.BlockSpec((1, LANES, CMP_HIDDEN), lambda a, i: (a, 0, 0)),
        ],
        out_specs=pl.BlockSpec((1, 1, N_KV_GROUPS, n, LANES), lambda a, i: (a, i, 0, 0, 0)),
        out_shape=jax.ShapeDtypeStruct((2, batch, N_KV_GROUPS, n, LANES), BF16),
        compiler_params=pltpu.CompilerParams(
            dimension_semantics=("arbitrary", "arbitrary"), vmem_limit_bytes=VMEM_LIMIT),
        name="compress",
    )(kvc, pos128, w1bd, b1, w2d, w2t)


def _attn_t_kernel(q_ref, ks_ref, kw_ref, vs_ref, vw_ref, ck_ref, cvt_ref, gate_ref, ovt_ref, ebt_ref,
                   o_ref, kks, kkw, vts, vtw, qst, sel_ref, oc_ref, m_s, m_w, acc_s, acc_w):
    g = pl.program_id(1)
    i = pl.program_id(2)
    seq = ks_ref.shape[0]
    nk = seq // TQ
    n_pairs = HEADS_PER_GROUP // 2

    @pl.when(i == 0)
    def _prep():
        lane = lax.broadcasted_iota(jnp.int32, (TQ, LANES), 1)
        mine = (lane >= HEAD_DIM).astype(jnp.int32) == g
        ones = jnp.ones((VT_ROWS - HEAD_DIM, TQ), F32)

        def both_halves(x):
            xr = jnp.concatenate([x[:, HEAD_DIM:], x[:, :HEAD_DIM]], axis=1)
            return jnp.where(mine, x, xr)

        for kt in range(nk):
            keys = slice(kt * TQ, (kt + 1) * TQ)
            kks[keys, :] = both_halves(ks_ref[keys, :])
            kkw[keys, :] = both_halves(kw_ref[keys, :])
            for src, dst in ((vs_ref, vts), (vw_ref, vtw)):
                vt = both_halves(src[keys, :]).astype(F32).T
                dst[kt] = jnp.concatenate([vt[:HEAD_DIM, :], ones], axis=0).astype(BF16)

    lane_q = lax.broadcasted_iota(jnp.int32, (TQ, LANES), 1)
    lo_q = lane_q < HEAD_DIM
    for p in range(n_pairs):
        qp = q_ref[:, p * LANES:(p + 1) * LANES]
        zero = jnp.zeros_like(qp)
        qst[p * TQ:(p + 1) * TQ, :] = jnp.where(lo_q, qp, zero)
        qst[(n_pairs + p) * TQ:(n_pairs + p + 1) * TQ, :] = jnp.where(lo_q, zero, qp)

    def q_block(hb):
        return qst[hb * TQ:(hb + 1) * TQ, :]

    cmp_tok = lax.broadcasted_iota(jnp.int32, (N_CMP_PAD, TQ), 0)
    tq_c = i * TQ + lax.broadcasted_iota(jnp.int32, (N_CMP_PAD, TQ), 1)
    cmp_mask = (cmp_tok * CMP_STRIDE + (CMP_BLOCK - 1)) <= tq_c
    ck = ck_ref[0, 0, 0]
    cvt = cvt_ref[0, 0, 0]
    def cmp_head(hb):
        s = jnp.where(cmp_mask, _dot_nt(ck, q_block(hb)), NEG_INF)
        m = jnp.max(s, axis=0, keepdims=True)
        e = jnp.where(cmp_mask, jnp.exp2(s - m), 0.0)
        l = jnp.sum(e, axis=0, keepdims=True)
        pc = e * jnp.where(l > 0.0, 1.0 / l, 0.0)
        oc_ref[hb] = _dot(cvt, pc.astype(BF16))[:HEAD_DIM, :]
        return pc

    def select_blocks():
        pcs = functools.reduce(lambda a, b: a + b, [cmp_head(hb) for hb in range(HEADS_PER_GROUP)])
        pcs_hi = pcs.astype(BF16)
        pcs_lo = (pcs - pcs_hi.astype(F32)).astype(BF16)
        imp = _dot(ovt_ref[...], pcs_hi) + _dot(ovt_ref[...], pcs_lo)
        n_blk = seq // SLC_BLOCK
        blk = lax.broadcasted_iota(jnp.int32, (n_blk, TQ), 0)
        tq_t = i * TQ + lax.broadcasted_iota(jnp.int32, (n_blk, TQ), 1)
        cur = lax.shift_right_logical(tq_t, SLC_BLOCK.bit_length() - 1)
        forced = (blk == 0) | (blk == cur) | (blk == cur - 1)
        valid = blk * SLC_BLOCK <= tq_t
        score = jnp.where(valid, imp + jnp.where(forced, FORCE_BONUS, 0.0), NEG_INF)
        blk_f = blk.astype(F32)
        sel_t = jnp.zeros((n_blk, TQ), F32)
        for _ in range(SLC_TOP_N):
            mx = jnp.max(score, axis=0, keepdims=True)
            first = jnp.min(jnp.where(score == mx, blk_f, float(n_blk)), axis=0, keepdims=True)
            pick = blk_f == first
            sel_t = jnp.where(pick, 1.0, sel_t)
            score = jnp.where(pick, -jnp.inf, score)
        sel_ref[...] = jnp.concatenate([sel_t, jnp.zeros((N_SLC_PAD - n_blk, TQ), F32)], axis=0).astype(BF16)

    key_row = lax.broadcasted_iota(jnp.int32, (TQ, TQ), 0)
    q_col = lax.broadcasted_iota(jnp.int32, (TQ, TQ), 1)
    causal_bias = jnp.where(key_row <= q_col, 0.0, NEG_INF)
    tail_bias = jnp.where(key_row > q_col, 0.0, NEG_INF)

    def run_steps(steps):
        def score_stage(ctx, hb):
            k_tile, _, _, m_ref, bias, first = ctx
            s = _dot_nt(k_tile, q_block(hb))
            if bias is not None:
                s = s + bias
            m_new = jnp.max(s, axis=0, keepdims=True)
            if first:
                alpha = None
            else:
                m_old = m_ref[hb]
                m_new = jnp.maximum(m_old, m_new)
                alpha = jnp.exp2(m_old - m_new)
            m_ref[hb] = m_new
            return jnp.exp2(s - m_new).astype(BF16), alpha

        def value_stage(ctx, hb, p, alpha):
            _, vt_ref, acc_ref, _, _, _ = ctx
            pv = _dot(vt_ref[0][vt_ref[1]], p)
            acc_ref[hb] = pv if alpha is None else alpha * acc_ref[hb] + pv

        pending = []
        for k_ref, vt_ref, acc_ref, m_ref, kt, bias_fn, first in steps:
            keys = pl.ds(pl.multiple_of(kt * TQ, TQ), TQ)
            ctx = (k_ref[keys, :], (vt_ref, kt), acc_ref, m_ref, bias_fn(), first)
            for hb in range(HEADS_PER_GROUP):
                pending.append((ctx, hb) + score_stage(ctx, hb))
                if len(pending) > SCORE_LOOKAHEAD:
                    value_stage(*pending.pop(0))
        for item in pending:
            value_stage(*item)

    def sel_bias(kt, extra=None):
        b = (_dot(ebt_ref[kt], sel_ref[...]) - 1.0) * (-NEG_INF)
        return b if extra is None else b + extra

    def win(kt, bias, first=False):
        return (kkw, vtw, acc_w, m_w, kt, lambda: bias, first)

    def sel(kt):
        return (kks, vts, acc_s, m_s, kt, lambda: sel_bias(kt), False)

    def first_block(all_window_tiles):
        def first_sel_bias():
            select_blocks()
            return sel_bias(i, causal_bias)

        steps = [win(i, causal_bias, True)]
        if all_window_tiles:
            steps.append(win(i - 1, None))
            steps.append(win(jnp.maximum(i - 2, 0), tail_bias + jnp.where(i >= 2, 0.0, NEG_INF)))
        steps.append((kks, vts, acc_s, m_s, i, first_sel_bias, True))
        run_steps(steps)

    pl.when(i == 0)(functools.partial(first_block, False))
    pl.when(i >= 1)(functools.partial(first_block, True))

    def sel_pair(j, carry):
        run_steps([sel(2 * j), sel(2 * j + 1)])
        return carry

    lax.fori_loop(0, lax.shift_right_logical(i, 1), sel_pair, 0)

    @pl.when((i & 1) == 1)
    def _sel_odd():
        run_steps([sel(i - 1)])

    gate_t = gate_ref[...].T

    def gate_row(head, br):
        r0 = head * N_BRANCH + br
        r1 = (HEADS_PER_GROUP + head) * N_BRANCH + br
        return jnp.where(g == 0, gate_t[r0:r0 + 1, :], gate_t[r1:r1 + 1, :])

    outs = []
    for head in range(HEADS_PER_GROUP):
        hb = (head % 2) * n_pairs + head // 2
        t = gate_row(head, 0) * oc_ref[hb]
        for br, acc_ref in ((1, acc_s), (2, acc_w)):
            acc = acc_ref[hb]
            t = t + (gate_row(head, br) * (1.0 / acc[HEAD_DIM:HEAD_DIM + 1, :])) * acc[:HEAD_DIM, :]
        outs.append(t)
    o_ref[...] = jnp.concatenate(outs, axis=0).T.astype(o_ref.dtype)


def _attention_t(q, ks, kw, vs, vw, cmp_kv, gate, ovt_mat, ebt_mat, batch, seq):
    nq = seq // TQ
    gw = HEADS_PER_GROUP * HEAD_DIM
    kv_spec = pl.BlockSpec((seq, LANES), lambda b, g, i: (b, 0))
    return pl.pallas_call(
        _attn_t_kernel,
        grid=(batch, N_KV_GROUPS, nq),
        in_specs=[
            pl.BlockSpec((TQ, gw), lambda b, g, i: (b * nq + i, g)),
            kv_spec, kv_spec, kv_spec, kv_spec,
            pl.BlockSpec((1, 1, 1, N_CMP_PAD, LANES), lambda b, g, i: (0, b, g, 0, 0)),
            pl.BlockSpec((1, 1, 1, LANES, N_CMP_PAD), lambda b, g, i: (1, b, g, 0, 0)),
            pl.BlockSpec((TQ, LANES), lambda b, g, i: (b * nq + i, 0)),
            pl.BlockSpec((seq // SLC_BLOCK, N_CMP_PAD), lambda b, g, i: (0, 0)),
            pl.BlockSpec((nq, TQ, N_SLC_PAD), lambda b, g, i: (0, 0, 0)),
        ],
        out_specs=pl.BlockSpec((TQ, gw), lambda b, g, i: (b * nq + i, g)),
        out_shape=jax.ShapeDtypeStruct((batch * seq, ATTN_WIDTH), BF16),
        scratch_shapes=[
            pltpu.VMEM((seq, LANES), BF16), pltpu.VMEM((seq, LANES), BF16),
            pltpu.VMEM((nq, VT_ROWS, TQ), BF16), pltpu.VMEM((nq, VT_ROWS, TQ), BF16),
            pltpu.VMEM((HEADS_PER_GROUP * TQ, LANES), BF16),
            pltpu.VMEM((N_SLC_PAD, TQ), BF16),
            pltpu.VMEM((HEADS_PER_GROUP, HEAD_DIM, TQ), F32),
            pltpu.VMEM((HEADS_PER_GROUP, 1, TQ), F32),
            pltpu.VMEM((HEADS_PER_GROUP, 1, TQ), F32),
            pltpu.VMEM((HEADS_PER_GROUP, VT_ROWS, TQ), F32),
            pltpu.VMEM((HEADS_PER_GROUP, VT_ROWS, TQ), F32),
        ],
        compiler_params=pltpu.CompilerParams(
            dimension_semantics=("arbitrary", "arbitrary", "arbitrary"), vmem_limit_bytes=VMEM_LIMIT),
        name="nsa_attention",
    )(q, ks, kw, vs, vw, cmp_kv, cmp_kv, gate, ovt_mat, ebt_mat)


def _merge_kernel(x_ref, up_ref, at_ref, mg_ref, pw_ref, ps_ref, wpb_ref, wab_ref, wo_ref,
                  o_ref, halo_ref, *, tiles_per_seq):
    i = pl.program_id(0)
    tm = x_ref.shape[0]
    u = up_ref[...]

    @pl.when((i % tiles_per_seq) == 0)
    def _seq_start():
        halo_ref[...] = jnp.zeros_like(halo_ref)

    halo = halo_ref[...]
    halo_ref[...] = u[tm - POOL_HALO:, :]
    ue = jnp.concatenate([halo, u], axis=0)
    rowp = lax.broadcasted_iota(jnp.int32, (tm, POOL_GROUP), 0) + (i % tiles_per_seq) * tm
    ys = []
    for gi, w in enumerate(POOL_WINDOWS):
        xg = ue[:, gi * POOL_GROUP:(gi + 1) * POOL_GROUP]
        sw = xg
        k = 1
        while k < w:
            sw = sw + pltpu.roll(sw, k, 0)
            k *= 2
        cnt = jnp.minimum(rowp + 1, w).astype(F32)
        ug = u[:, gi * POOL_GROUP:(gi + 1) * POOL_GROUP]
        pooled = sw[POOL_HALO:, :] / cnt - ug
        ys.append(_dot(pooled.astype(BF16), pw_ref[gi]))
    y = jnp.concatenate(ys, axis=1) * ps_ref[...]
    y_pool = _dot(y.astype(BF16), wpb_ref[...])
    y_attn = _dot(at_ref[...], wab_ref[...])
    mg = mg_ref[...]
    merged = mg[:, :D_MODEL] * y_pool + mg[:, D_MODEL:] * y_attn
    o_ref[...] = x_ref[...] + _dot(merged.astype(BF16), wo_ref[...])


def _merge(x2, u_pool, attn, mg, pool_w, pool_scale, w_pool_br, w_attn_br, w_o, seq):
    t = x2.shape[0]
    row = lambda i: (i, 0)
    const = lambda i: (0, 0)
    return pl.pallas_call(
        functools.partial(_merge_kernel, tiles_per_seq=seq // TM),
        grid=(t // TM,),
        in_specs=[
            pl.BlockSpec((TM, D_MODEL), row),
            pl.BlockSpec((TM, POOL_WIDTH), row),
            pl.BlockSpec((TM, ATTN_WIDTH), row),
            pl.BlockSpec((TM, 2 * D_MODEL), row),
            pl.BlockSpec((N_POOL_GROUPS, POOL_GROUP, POOL_GROUP), lambda i: (0, 0, 0)),
            pl.BlockSpec((1, POOL_WIDTH), const),
            pl.BlockSpec((POOL_WIDTH, D_MODEL), const),
            pl.BlockSpec((ATTN_WIDTH, D_MODEL), const),
            pl.BlockSpec((D_MODEL, D_MODEL), const),
        ],
        out_specs=pl.BlockSpec((TM, D_MODEL), row),
        out_shape=jax.ShapeDtypeStruct((t, D_MODEL), F32),
        scratch_shapes=[pltpu.VMEM((POOL_HALO, POOL_WIDTH), F32)],
        compiler_params=pltpu.CompilerParams(
            dimension_semantics=("arbitrary",), vmem_limit_bytes=VMEM_LIMIT),
        name="pool_merge",
    )(x2, u_pool, attn, mg, pool_w, pool_scale, w_pool_br, w_attn_br, w_o)


def _ffn_kernel(x_ref, nw_ref, wup_ref, cw_ref, cb_ref, wdn_ref, o_ref, halo_ref, act_ref, *, tiles_per_seq):
    i = pl.program_id(0)
    tm = x_ref.shape[0]
    x = x_ref[...]
    ms = jnp.mean(x * x, axis=-1, keepdims=True)
    h = ((x * lax.rsqrt(ms + EPS)) * nw_ref[...]).astype(BF16)

    @pl.when((i % tiles_per_seq) == 0)
    def _seq_start():
        halo_ref[...] = jnp.zeros_like(halo_ref)

    def conv(c):
        u = _dot(h, wup_ref[c])
        halo = halo_ref[c]
        halo_ref[c] = u[tm - CONV_HALO:, :]
        ue = jnp.concatenate([halo, u], axis=0)
        w = cw_ref[c]
        u1 = pltpu.roll(ue, 1, 0)[CONV_HALO:, :]
        u2 = pltpu.roll(ue, 2, 0)[CONV_HALO:, :]
        return cb_ref[c] + w[0:1, :] * u2 + w[1:2, :] * u1 + w[2:3, :] * u

    for c in range(N_FF_CHUNKS):
        gate = conv(c)
        val = conv(N_FF_CHUNKS + c)
        act = (gate * jax.nn.sigmoid(gate)) * val
        act_ref[:, c * FF_CHUNK:(c + 1) * FF_CHUNK] = act.astype(BF16)
    o_ref[...] = x + _dot(act_ref[...], wdn_ref[...])


def _ffn(x1, nw, wup, cw, cb, wdn, seq):
    t = x1.shape[0]
    row = lambda i: (i, 0)
    c3 = lambda i: (0, 0, 0)
    return pl.pallas_call(
        functools.partial(_ffn_kernel, tiles_per_seq=seq // TM),
        grid=(t // TM,),
        in_specs=[
            pl.BlockSpec((TM, D_MODEL), row),
            pl.BlockSpec((1, D_MODEL), lambda i: (0, 0)),
            pl.BlockSpec((2 * N_FF_CHUNKS, D_MODEL, FF_CHUNK), c3),
            pl.BlockSpec((2 * N_FF_CHUNKS, CONV_HALO, FF_CHUNK), c3),
            pl.BlockSpec((2 * N_FF_CHUNKS, 1, FF_CHUNK), c3),
            pl.BlockSpec((D_FF_PAD, D_MODEL), lambda i: (0, 0)),
        ],
        out_specs=pl.BlockSpec((TM, D_MODEL), row),
        out_shape=jax.ShapeDtypeStruct((t, D_MODEL), F32),
        scratch_shapes=[pltpu.VMEM((2 * N_FF_CHUNKS, CONV_HALO, FF_CHUNK), F32),
                        pltpu.VMEM((TM, D_FF_PAD), BF16)],
        compiler_params=pltpu.CompilerParams(
            dimension_semantics=("arbitrary",), vmem_limit_bytes=VMEM_LIMIT),
        name="conv_ffn",
    )(x1, nw, wup, cw, cb, wdn)


def _w_in_columns():
    ranges = [(0, OFF_KV)]
    for kv in range(2):
        for br in range(N_BRANCH):
            base = OFF_KV + br * 4 * HEAD_DIM + kv * 2 * HEAD_DIM
            ranges.append((base, base + 2 * HEAD_DIM))
    ranges.append((OFF_MERGE_G, OFF_MERGE_G + 2 * D_MODEL))
    ranges.append((OFF_NSA_G, OFF_MERGE_G))
    return ranges


def _rope_tables(seq):
    half = HEAD_DIM // 2
    freqs = ROPE_THETA ** (-jnp.arange(half, dtype=F32) / half)
    ang = jnp.arange(seq, dtype=F32)[:, None] * freqs[None, :]
    cos = jnp.cos(ang)
    sin = jnp.sin(ang)
    zero = jnp.zeros_like(sin)
    cos_t = jnp.tile(cos, (1, LANES // half))
    sin_a = jnp.tile(jnp.concatenate([-sin, zero], axis=1), (1, LANES // HEAD_DIM))
    sin_b = jnp.tile(jnp.concatenate([zero, sin], axis=1), (1, LANES // HEAD_DIM))
    return cos_t, sin_a, sin_b


def _selection_constants(seq):
    n_cmp = (seq - CMP_BLOCK) // CMP_STRIDE + 1
    n_slc = seq // SLC_BLOCK
    ci = np.arange(N_CMP_PAD)[:, None]
    sj = np.arange(N_SLC_PAD)[None, :]
    ov = ((ci * CMP_STRIDE < (sj + 1) * SLC_BLOCK) & (ci * CMP_STRIDE + CMP_BLOCK > sj * SLC_BLOCK)
          & (ci < n_cmp) & (sj < n_slc))
    ovt = ov.T[:n_slc]
    ebt = (np.arange(seq)[:, None] // SLC_BLOCK) == np.arange(N_SLC_PAD)[None, :]
    ebt = ebt.reshape(seq // TQ, TQ, N_SLC_PAD)
    return jnp.asarray(ovt, BF16), jnp.asarray(ebt, BF16)


def _chunk_cols(w, n_chunks):
    return jnp.moveaxis(w.reshape(w.shape[:-1] + (n_chunks, FF_CHUNK)), -2, 0)


def _layer(x, attn_norm_w, w_in, pool_w, pool_scale, q_norm_w, k_norm_w, cmp_pos, cmp_w1, cmp_b1, cmp_w2,
           w_pool_br, w_attn_br, w_o, ffn_norm_w, w_up, conv_w, conv_b, w_down):
    batch, seq, _ = x.shape
    t = batch * seq
    x2 = x.reshape(t, D_MODEL)

    w_perm = jnp.concatenate([w_in[:, a:b] for a, b in _w_in_columns()], axis=1)
    w_perm = jnp.pad(w_perm, ((0, 0), (0, _N_SLABS * LANES - w_perm.shape[1]))).astype(BF16)
    normw = jnp.concatenate([jnp.tile(q_norm_w[None, :], (1, 2)), jnp.tile(k_norm_w, (1, 2))], axis=0)
    cos_t, sin_a, sin_b = _rope_tables(seq)
    u_pool, q, kvc, ks, kw, vs, vw, mg, gate = _inproj(
        x2, attn_norm_w[None, :], w_perm, normw, cos_t, sin_a, sin_b, seq)

    w1 = cmp_w1.reshape(2, CMP_BLOCK, HEAD_DIM, CMP_HIDDEN)
    zero = jnp.zeros_like(w1)
    w1bd = jnp.concatenate([jnp.concatenate([w1, zero], axis=-1), jnp.concatenate([zero, w1], axis=-1)],
                           axis=2).astype(BF16)
    pos128 = jnp.tile(cmp_pos, (1, 1, N_KV_GROUPS))
    b1 = jnp.tile(cmp_b1, (1, N_KV_GROUPS))[:, None, :]
    w2d = jnp.concatenate([cmp_w2, cmp_w2], axis=-1).astype(BF16)
    cmp_kv = _compress(kvc, pos128, w1bd, b1, w2d, batch, seq)

    ovt_mat, ebt_mat = _selection_constants(seq)
    attn = _attention_t(q, ks, kw, vs, vw, cmp_kv, gate, ovt_mat, ebt_mat, batch, seq)

    x1 = _merge(x2, u_pool, attn, mg, pool_w.astype(BF16), pool_scale[None, :], w_pool_br.astype(BF16),
                w_attn_br.astype(BF16), w_o.astype(BF16), seq)

    pad = D_FF_PAD - D_FF

    def pad_halves(w):
        lead = [(0, 0)] * (w.ndim - 1)
        return jnp.concatenate([jnp.pad(w[..., :D_FF], lead + [(0, pad)]),
                                jnp.pad(w[..., D_FF:], lead + [(0, pad)])], axis=-1)

    wup = _chunk_cols(pad_halves(w_up).astype(BF16), 2 * N_FF_CHUNKS)
    cw = _chunk_cols(jnp.pad(pad_halves(conv_w), ((0, CONV_HALO - CONV_WIDTH), (0, 0))), 2 * N_FF_CHUNKS)
    cb = _chunk_cols(pad_halves(conv_b)[None, :], 2 * N_FF_CHUNKS)
    wdn = jnp.pad(w_down, ((0, pad), (0, 0))).astype(BF16)
    out = _ffn(x1, ffn_norm_w[None, :], wup, cw, cb, wdn, seq)
    return out.reshape(batch, seq, D_MODEL)


def kernel(x, attn_norm_w, w_in, pool_w, pool_scale, q_norm_w, k_norm_w, cmp_pos, cmp_w1, cmp_b1, cmp_w2,
           w_pool_br, w_attn_br, w_o, ffn_norm_w, w_up, conv_w, conv_b, w_down):
    for l in range(attn_norm_w.shape[0]):
        x = _layer(x, attn_norm_w[l], w_in[l], pool_w[l], pool_scale[l], q_norm_w[l], k_norm_w[l],
                   cmp_pos[l], cmp_w1[l], cmp_b1[l], cmp_w2[l], w_pool_br[l], w_attn_br[l], w_o[l],
                   ffn_norm_w[l], w_up[l], conv_w[l], conv_b[l], w_down[l])
    return x
```
